```python
import jax, jax.numpy as jnp
from jax import lax
import numpy as np

D_MODEL = 1024
BATCH = 8
SEQ = 4096
DEPTH = 1
DEC_BATCH = 128
DEC_SEQ = 1
PAST_LEN = 16384
PAGE_SIZE = 128

HEAD_DIM = 64
SWA_HEADS = 8
SWA_KV_HEADS = 2
SWA_GROUP = SWA_HEADS // SWA_KV_HEADS
WINDOW = 128
SWA_BLOCK = WINDOW
N_MEM = 256
MEM_HEADS = 4
RNN_WIDTH = 256
RNN_BLOCKS = 4
RNN_BLOCK_DIM = RNN_WIDTH // RNN_BLOCKS
CONV_WIDTH = 4
RG_C = 8.0
D_FF = 2816
EPS = 1e-6
NEG = -1e30

Q_SWA = SWA_HEADS * HEAD_DIM
KV_SWA = SWA_KV_HEADS * HEAD_DIM
Q_MEM = MEM_HEADS * HEAD_DIM
D_IN = Q_SWA + 2 * KV_SWA + Q_MEM + 2 * RNN_WIDTH
D_MIX = Q_SWA + Q_MEM + RNN_WIDTH
IN_SPLITS = (Q_SWA, Q_SWA + KV_SWA, Q_SWA + 2 * KV_SWA, Q_SWA + 2 * KV_SWA + Q_MEM,
             Q_SWA + 2 * KV_SWA + Q_MEM + RNN_WIDTH)

kernel_name = 'hymba_swa_rglru_memxattn_macaron_step'


def rms_norm(x, g):
    xf = x.astype(jnp.float32)
    y = xf * lax.rsqrt(jnp.mean(xf * xf, axis=-1, keepdims=True) + EPS)
    return (y * g.astype(jnp.float32)).astype(x.dtype)


def swiglu(x, w_in, w_out):
    gate, up = jnp.split(x @ w_in, 2, axis=-1)
    return (jax.nn.silu(gate) * up) @ w_out


def sink_softmax(s, sink, mask):
    s = jnp.where(mask, s, NEG)
    sink = sink.astype(jnp.float32)[:, :, None, None]
    lse = jnp.logaddexp(jax.nn.logsumexp(s, axis=-1, keepdims=True), sink)
    return jnp.exp(s - lse)


def swa_prompt(q, k, v, sinks):
    B, T = q.shape[:2]
    nb = T // SWA_BLOCK
    S = SWA_BLOCK
    qb = q.reshape(B, nb, S, SWA_KV_HEADS, SWA_GROUP, HEAD_DIM)
    kb = k.reshape(B, nb, S, SWA_KV_HEADS, HEAD_DIM)
    vb = v.reshape(B, nb, S, SWA_KV_HEADS, HEAD_DIM)
    pad = jnp.zeros_like(kb[:, :1])
    kk = jnp.concatenate([jnp.concatenate([pad, kb[:, :-1]], axis=1), kb], axis=2)
    vv = jnp.concatenate([jnp.concatenate([pad, vb[:, :-1]], axis=1), vb], axis=2)
    s = jnp.einsum('bnqkgd,bnskd->bnkgqs', qb, kk).astype(jnp.float32) * (HEAD_DIM ** -0.5)
    rel = (jnp.arange(S)[:, None] + S) - jnp.arange(2 * S)[None, :]
    band = (rel >= 0) & (rel <= WINDOW)
    blk_ok = (jnp.arange(nb)[:, None, None] > 0) | (jnp.arange(2 * S)[None, None, :] >= S)
    mask = (band[None] & blk_ok)[None, :, None, None]
    p = sink_softmax(s, sinks.reshape(SWA_KV_HEADS, SWA_GROUP), mask)
    o = jnp.einsum('bnkgqs,bnskd->bnqkgd', p.astype(v.dtype), vv)
    return o.reshape(B, T, Q_SWA), k[:, -WINDOW:], v[:, -WINDOW:]


def swa_sample(q, k, v, ck, cv, sinks):
    B, T = q.shape[:2]
    kk = jnp.concatenate([ck, k], axis=1)
    vv = jnp.concatenate([cv, v], axis=1)
    qg = q.reshape(B, T, SWA_KV_HEADS, SWA_GROUP, HEAD_DIM)
    s = jnp.einsum('bqkgd,bskd->bkgqs', qg, kk).astype(jnp.float32) * (HEAD_DIM ** -0.5)
    rel = (WINDOW + jnp.arange(T))[:, None] - jnp.arange(WINDOW + T)[None, :]
    mask = (rel >= 0) & (rel <= WINDOW)
    p = sink_softmax(s, sinks.reshape(SWA_KV_HEADS, SWA_GROUP), mask)
    o = jnp.einsum('bkgqs,bskd->bqkgd', p.astype(v.dtype), vv)
    return o.reshape(B, T, Q_SWA), kk[:, -WINDOW:], vv[:, -WINDOW:]


def mem_attend(q, mk, mv):
    s = jnp.einsum('bthd,bmhd->bhtm', q, mk).astype(jnp.float32) * (HEAD_DIM ** -0.5)
    p = jax.nn.softmax(s, axis=-1)
    o = jnp.einsum('bhtm,bmhd->bthd', p.astype(mv.dtype), mv)
    return o.reshape(q.shape[0], q.shape[1], Q_MEM)


def causal_conv(xr, buf, w, b):
    T = xr.shape[1]
    xc = jnp.concatenate([buf, xr], axis=1)
    y = b + sum(xc[:, j:j + T] * w[j] for j in range(CONV_WIDTH))
    return y, xc[:, -(CONV_WIDTH - 1):]


def linear_scan(a, b, h0):
    b = b.at[:, 0].add(a[:, 0] * h0)

    def combine(left, right):
        return (left[0] * right[0], right[0] * left[1] + right[1])

    _, h = lax.associative_scan(combine, (a, b), axis=1)
    return h


def rg_lru(u, h0, wa, ba, wx, bx, lam):
    B, T, _ = u.shape
    ub = u.reshape(B, T, RNN_BLOCKS, RNN_BLOCK_DIM)
    r = jax.nn.sigmoid(jnp.einsum('btnd,nde->btne', ub, wa) + ba).reshape(B, T, RNN_WIDTH)
    i = jax.nn.sigmoid(jnp.einsum('btnd,nde->btne', ub, wx) + bx).reshape(B, T, RNN_WIDTH)
    log_a = -RG_C * r.astype(jnp.float32) * jax.nn.softplus(-lam.astype(jnp.float32))
    a = jnp.exp(log_a)
    bt = jnp.sqrt(-jnp.expm1(2.0 * log_a)) * (i * u).astype(jnp.float32)
    h = linear_scan(a, bt, h0.astype(jnp.float32))
    return h.astype(u.dtype), h[:, -1].astype(h0.dtype)


def decoder_layer(x, lp, mk, mv, conv_buf, h0, swa_cache):
    B, T, _ = x.shape
    x = x + 0.5 * rms_norm(swiglu(rms_norm(x, lp['ln_ffn1_pre']), lp['w_ffn1_in'], lp['w_ffn1_out']), lp['ln_ffn1_post'])
    xn = rms_norm(x, lp['ln_mix_pre'])
    q_swa, k_swa, v_swa, q_mem, x_rnn, g_rnn = jnp.split(xn @ lp['w_in'], IN_SPLITS, axis=-1)
    q_swa = q_swa.reshape(B, T, SWA_HEADS, HEAD_DIM)
    k_swa = k_swa.reshape(B, T, SWA_KV_HEADS, HEAD_DIM)
    v_swa = v_swa.reshape(B, T, SWA_KV_HEADS, HEAD_DIM)
    if swa_cache is None:
        o_swa, nk, nv = swa_prompt(q_swa, k_swa, v_swa, lp['swa_sinks'])
    else:
        o_swa, nk, nv = swa_sample(q_swa, k_swa, v_swa, swa_cache[0], swa_cache[1], lp['swa_sinks'])
    o_mem = mem_attend(q_mem.reshape(B, T, MEM_HEADS, HEAD_DIM), mk, mv)
    u, new_conv = causal_conv(x_rnn, conv_buf, lp['conv_w'], lp['conv_b'])
    h, h_last = rg_lru(u, h0, lp['rg_wa'], lp['rg_ba'], lp['rg_wx'], lp['rg_bx'], lp['rg_lambda'])
    o_rnn = h * jax.nn.gelu(g_rnn)
    o = jnp.concatenate([o_swa, o_mem, o_rnn], axis=-1) @ lp['w_out']
    x = x + rms_norm(o, lp['ln_mix_post'])
    x = x + 0.5 * rms_norm(swiglu(rms_norm(x, lp['ln_ffn2_pre']), lp['w_ffn2_in'], lp['w_ffn2_out']), lp['ln_ffn2_post'])
    return x, nk, nv, new_conv, h_last


def setup_inputs(seed: int = 0) -> dict:
    key = jax.random.key(seed)
    ks = jax.random.split(key, 40)
    nrm = lambda k, shape, scale=1.0: scale * jax.random.normal(k, shape, jnp.float32)
    gain = lambda k: 1.0 + nrm(k, (DEPTH, D_MODEL), 0.02)
    a_init = jax.random.uniform(ks[30], (DEPTH, RNN_WIDTH), jnp.float32, 0.9, 0.999)
    sig = a_init ** (1.0 / RG_C)
    rg_lambda = jnp.log(sig) - jnp.log1p(-sig)
    return {
        'x_prompt': nrm(ks[0], (BATCH, SEQ, D_MODEL)),
        'x_sample': nrm(ks[1], (DEC_BATCH, DEC_SEQ, D_MODEL)),
        'mem_prompt': nrm(ks[2], (BATCH, N_MEM, D_MODEL)),
        'cache_swa_k': nrm(ks[3], (DEPTH, DEC_BATCH, WINDOW, SWA_KV_HEADS, HEAD_DIM)),
        'cache_swa_v': nrm(ks[4], (DEPTH, DEC_BATCH, WINDOW, SWA_KV_HEADS, HEAD_DIM)),
        'cache_mem_k': nrm(ks[5], (DEPTH, DEC_BATCH, N_MEM, MEM_HEADS, HEAD_DIM)),
        'cache_mem_v': nrm(ks[6], (DEPTH, DEC_BATCH, N_MEM, MEM_HEADS, HEAD_DIM)),
        'state_conv': nrm(ks[7], (DEPTH, DEC_BATCH, CONV_WIDTH - 1, RNN_WIDTH)),
        'state_rglru_h': nrm(ks[8], (DEPTH, DEC_BATCH, RNN_WIDTH), 0.5),
        'ln_ffn1_pre': gain(ks[9]),
        'ln_ffn1_post': gain(ks[10]),
        'w_ffn1_in': nrm(ks[11], (DEPTH, D_MODEL, 2 * D_FF), D_MODEL ** -0.5),
        'w_ffn1_out': nrm(ks[12], (DEPTH, D_FF, D_MODEL), D_FF ** -0.5),
        'ln_mix_pre': gain(ks[13]),
        'ln_mix_post': gain(ks[14]),
        'w_in': nrm(ks[15], (DEPTH, D_MODEL, D_IN), D_MODEL ** -0.5),
        'w_out': nrm(ks[16], (DEPTH, D_MIX, D_MODEL), D_MIX ** -0.5),
        'swa_sinks': nrm(ks[17], (DEPTH, SWA_HEADS), 0.5),
        'conv_w': nrm(ks[18], (DEPTH, CONV_WIDTH, RNN_WIDTH), CONV_WIDTH ** -0.5),
        'conv_b': nrm(ks[19], (DEPTH, RNN_WIDTH), 0.01),
        'rg_wa': nrm(ks[20], (DEPTH, RNN_BLOCKS, RNN_BLOCK_DIM, RNN_BLOCK_DIM), RNN_BLOCK_DIM ** -0.5),
        'rg_ba': nrm(ks[21], (DEPTH, RNN_BLOCKS, RNN_BLOCK_DIM), 0.01),
        'rg_wx': nrm(ks[22], (DEPTH, RNN_BLOCKS, RNN_BLOCK_DIM, RNN_BLOCK_DIM), RNN_BLOCK_DIM ** -0.5),
        'rg_bx': nrm(ks[23], (DEPTH, RNN_BLOCKS, RNN_BLOCK_DIM), 0.01),
        'rg_lambda': rg_lambda,
        'ln_mem': gain(ks[24]),
        'w_mem_kv': nrm(ks[25], (DEPTH, D_MODEL, 2 * Q_MEM), D_MODEL ** -0.5),
        'ln_ffn2_pre': gain(ks[26]),
        'ln_ffn2_post': gain(ks[27]),
        'w_ffn2_in': nrm(ks[28], (DEPTH, D_MODEL, 2 * D_FF), D_MODEL ** -0.5),
        'w_ffn2_out': nrm(ks[29], (DEPTH, D_FF, D_MODEL), D_FF ** -0.5),
    }


def reference(x_prompt, x_sample, mem_prompt, cache_swa_k, cache_swa_v, cache_mem_k, cache_mem_v,
              state_conv, state_rglru_h, ln_ffn1_pre, ln_ffn1_post, w_ffn1_in, w_ffn1_out,
              ln_mix_pre, ln_mix_post, w_in, w_out, swa_sinks, conv_w, conv_b, rg_wa, rg_ba,
              rg_wx, rg_bx, rg_lambda, ln_mem, w_mem_kv, ln_ffn2_pre, ln_ffn2_post,
              w_ffn2_in, w_ffn2_out):
    B = x_prompt.shape[0]
    yp, ys = x_prompt, x_sample
    skp, svp, mkp, mvp, cvp, hp = [], [], [], [], [], []
    sks, svs, cvs, hs = [], [], [], []
    for l in range(DEPTH):
        lp = dict(ln_ffn1_pre=ln_ffn1_pre[l], ln_ffn1_post=ln_ffn1_post[l],
                  w_ffn1_in=w_ffn1_in[l], w_ffn1_out=w_ffn1_out[l],
                  ln_mix_pre=ln_mix_pre[l], ln_mix_post=ln_mix_post[l],
                  w_in=w_in[l], w_out=w_out[l], swa_sinks=swa_sinks[l],
                  conv_w=conv_w[l], conv_b=conv_b[l], rg_wa=rg_wa[l], rg_ba=rg_ba[l],
                  rg_wx=rg_wx[l], rg_bx=rg_bx[l], rg_lambda=rg_lambda[l],
                  ln_ffn2_pre=ln_ffn2_pre[l], ln_ffn2_post=ln_ffn2_post[l],
                  w_ffn2_in=w_ffn2_in[l], w_ffn2_out=w_ffn2_out[l])
        mkv = (rms_norm(mem_prompt, ln_mem[l]) @ w_mem_kv[l]).reshape(B, N_MEM, 2, MEM_HEADS, HEAD_DIM)
        mk_p, mv_p = mkv[:, :, 0], mkv[:, :, 1]
        conv0 = jnp.zeros((B, CONV_WIDTH - 1, RNN_WIDTH), x_prompt.dtype)
        h0 = jnp.zeros((B, RNN_WIDTH), x_prompt.dtype)
        yp, nk, nv, nc, nh = decoder_layer(yp, lp, mk_p, mv_p, conv0, h0, None)
        skp.append(nk); svp.append(nv); mkp.append(mk_p); mvp.append(mv_p); cvp.append(nc); hp.append(nh)
        ys, nk, nv, nc, nh = decoder_layer(ys, lp, cache_mem_k[l], cache_mem_v[l], state_conv[l],
                                           state_rglru_h[l], (cache_swa_k[l], cache_swa_v[l]))
        sks.append(nk); svs.append(nv); cvs.append(nc); hs.append(nh)
    return (yp, ys,
            jnp.stack(skp), jnp.stack(svp), jnp.stack(mkp), jnp.stack(mvp), jnp.stack(cvp), jnp.stack(hp),
            jnp.stack(sks), jnp.stack(svs), jnp.stack(cvs), jnp.stack(hs))
```

```python
import functools

import jax
import jax.numpy as jnp
from jax import lax
from jax.experimental import pallas as pl
from jax.experimental.pallas import tpu as pltpu

F32 = jnp.float32
BF16 = jnp.bfloat16

D_MODEL = 1024
HEAD_DIM = 64
SWA_HEADS = 8
SWA_KV_HEADS = 2
SWA_GROUP = SWA_HEADS // SWA_KV_HEADS
WINDOW = 128
N_MEM = 256
MEM_HEADS = 4
RNN_WIDTH = 256
RNN_BLOCKS = 4
RNN_BLOCK_DIM = RNN_WIDTH // RNN_BLOCKS
CONV_WIDTH = 4
RG_C = 8.0
D_FF = 2816
EPS = 1e-6
NEG = -1e30

Q_SWA = SWA_HEADS * HEAD_DIM
KV_SWA = SWA_KV_HEADS * HEAD_DIM
Q_MEM = MEM_HEADS * HEAD_DIM
D_IN = Q_SWA + 2 * KV_SWA + Q_MEM + 2 * RNN_WIDTH
D_MIX = Q_SWA + Q_MEM + RNN_WIDTH
C_K = Q_SWA
C_V = C_K + KV_SWA
C_QM = C_V + KV_SWA
C_XR = C_QM + Q_MEM
C_GR = C_XR + RNN_WIDTH
SCALE = HEAD_DIM ** -0.5

LANES = 128
VMEM_LIMIT = 56 * 1024 * 1024

FFN_TM = 256
MIX_TQ = 256
SAMPLE_SB = 16


def _dot(a, b):
    return jnp.dot(a, b, preferred_element_type=F32)


def _dot_nt(a, b):
    return lax.dot_general(a, b, (((1,), (1,)), ((), ())), preferred_element_type=F32)


def _rms(x, g):
    return x * lax.rsqrt(jnp.mean(x * x, axis=-1, keepdims=True) + EPS) * g


def _softplus(x):
    return jnp.maximum(x, 0.0) + jnp.log1p(jnp.exp(-jnp.abs(x)))


def _const_spec(shape):
    return pl.BlockSpec(shape, lambda *_: (0,) * len(shape), pipeline_mode=pl.Buffered(1))


def _params(n_grid):
    return pltpu.CompilerParams(dimension_semantics=("arbitrary",) * n_grid,
                                vmem_limit_bytes=VMEM_LIMIT)


def _ffn_kernel(x_ref, gpre_ref, gpost_ref, win_ref, wout_ref, o_ref):
    x = x_ref[...]
    xn = _rms(x, gpre_ref[...]).astype(BF16)
    gate = _dot(xn, win_ref[:, :D_FF])
    up = _dot(xn, win_ref[:, D_FF:])
    act = (gate * jax.nn.sigmoid(gate) * up).astype(BF16)
    y = _dot(act, wout_ref[...])
    o_ref[...] = x + 0.5 * _rms(y, gpost_ref[...])


def _ffn(x, g_pre, g_post, w_in, w_out, tm):
    m = x.shape[0]
    row = pl.BlockSpec((tm, D_MODEL), lambda i: (i, 0))
    return pl.pallas_call(
        _ffn_kernel,
        grid=(m // tm,),
        in_specs=[row, _const_spec((1, D_MODEL)), _const_spec((1, D_MODEL)),
                  _const_spec((D_MODEL, 2 * D_FF)), _const_spec((D_FF, D_MODEL))],
        out_specs=row,
        out_shape=jax.ShapeDtypeStruct((m, D_MODEL), F32),
        compiler_params=_params(1),
        name="ffn",
    )(x, g_pre, g_post, w_in, w_out)


def _memkv_kernel(m_ref, g_ref, w_ref, k_ref, v_ref):
    xn = _rms(m_ref[...], g_ref[...]).astype(BF16)
    kv = _dot(xn, w_ref[...])
    k_ref[...] = kv[:, :Q_MEM]
    v_ref[...] = kv[:, Q_MEM:]


def _memkv(mem, g, w):
    m = mem.shape[0]
    tm = 512
    out = pl.BlockSpec((tm, Q_MEM), lambda i: (i, 0))
    return pl.pallas_call(
        _memkv_kernel,
        grid=(m // tm,),
        in_specs=[pl.BlockSpec((tm, D_MODEL), lambda i: (i, 0)),
                  _const_spec((1, D_MODEL)), _const_spec((D_MODEL, 2 * Q_MEM))],
        out_specs=[out, out],
        out_shape=[jax.ShapeDtypeStruct((m, Q_MEM), F32)] * 2,
        compiler_params=_params(1),
        name="memkv",
    )(mem, g, w)


def _rglru_gates(u, wa_ref, ba_ref, wx_ref, bx_ref, lam_ref):
    ub = u.astype(BF16)
    r = jax.nn.sigmoid(_dot(ub, wa_ref[...]) + ba_ref[...])
    i = jax.nn.sigmoid(_dot(ub, wx_ref[...]) + bx_ref[...])
    log_a = -RG_C * r * _softplus(-lam_ref[...])
    a = jnp.exp(log_a)
    b = jnp.sqrt(1.0 - a * a) * (i * u)
    return a, b


def _head_mask(width, head):
    lane = lax.broadcasted_iota(jnp.int32, (1, width), 1)
    return (lane >= head * HEAD_DIM) & (lane < (head + 1) * HEAD_DIM)


def _mix_prompt_kernel(sinks_ref, x_ref, mk_ref, mv_ref, gpre_ref, win_ref, wout_ref, gpost_ref,
                       convw_ref, convb_ref, wa_ref, ba_ref, wx_ref, bx_ref, lam_ref,
                       y_ref, klast_ref, vlast_ref, convlast_ref, hlast_ref,
                       kd_scr, vd_scr, mkt_scr, mvb_scr, xr_scr, h_scr, ocat_scr):
    t = pl.program_id(1)
    tq = x_ref.shape[0]
    nblk = tq // WINDOW

    @pl.when(t == 0)
    def _():
        kd_scr[:, 0:WINDOW, :] = jnp.zeros((SWA_KV_HEADS, WINDOW, LANES), BF16)
        vd_scr[:, 0:WINDOW, :] = jnp.zeros((SWA_KV_HEADS, WINDOW, LANES), BF16)
        xr_scr[0:8, :] = jnp.zeros((8, RNN_WIDTH), F32)
        h_scr[...] = jnp.zeros((1, RNN_WIDTH), F32)
        mkt_scr[...] = mk_ref[...].T.astype(BF16)
        mvb_scr[...] = mv_ref[...].astype(BF16)

    x = x_ref[...]
    xn = _rms(x, gpre_ref[...]).astype(BF16)
    proj = _dot(xn, win_ref[...])

    k = proj[:, C_K:C_V]
    v = proj[:, C_V:C_QM]
    klast_ref[...] = k[tq - WINDOW:, :]
    vlast_ref[...] = v[tq - WINDOW:, :]
    lane = lax.broadcasted_iota(jnp.int32, (1, LANES), 1)
    lo = lane < HEAD_DIM
    kr = pltpu.roll(k, HEAD_DIM, 1)
    vr = pltpu.roll(v, HEAD_DIM, 1)
    kd_scr[0, WINDOW:, :] = jnp.where(lo, k, kr).astype(BF16)
    kd_scr[1, WINDOW:, :] = jnp.where(lo, kr, k).astype(BF16)
    vd_scr[0, WINDOW:, :] = jnp.where(lo, v, vr).astype(BF16)
    vd_scr[1, WINDOW:, :] = jnp.where(lo, vr, v).astype(BF16)

    rows = SWA_GROUP * WINDOW
    qi = lax.broadcasted_iota(jnp.int32, (rows, 2 * WINDOW), 0) & (WINDOW - 1)
    ki = lax.broadcasted_iota(jnp.int32, (rows, 2 * WINDOW), 1)
    band = (ki >= qi) & (ki <= qi + WINDOW)
    own_block = ki >= WINDOW

    for j in range(nblk):
        r0 = j * WINDOW
        if j == 0:
            mask = band & (own_block | (t > 0))
        else:
            mask = band
        for kv in range(SWA_KV_HEADS):
            q_parts, sink_parts = [], []
            for g in range(SWA_GROUP):
                head = kv * SWA_GROUP + g
                c0 = (head // 2) * LANES
                qt = proj[r0:r0 + WINDOW, c0:c0 + LANES] * SCALE
                keep = lo if head % 2 == 0 else jnp.logical_not(lo)
                q_parts.append(jnp.where(keep, qt, 0.0).astype(BF16))
                sink_parts.append(jnp.full((WINDOW, 1), sinks_ref[head], F32))
            q = jnp.concatenate(q_parts, axis=0)
            sink = jnp.concatenate(sink_parts, axis=0)
            s = _dot_nt(q, kd_scr[kv, r0:r0 + 2 * WINDOW, :])
            s = jnp.where(mask, s, NEG)
            m = jnp.maximum(jnp.max(s, axis=-1, keepdims=True), sink)
            e = jnp.exp(s - m)
            l = jnp.sum(e, axis=-1, keepdims=True) + jnp.exp(sink - m)
            o = _dot(e.astype(BF16), vd_scr[kv, r0:r0 + 2 * WINDOW, :]) / l
            for pair in range(SWA_GROUP // 2):
                even = o[(2 * pair) * WINDOW:(2 * pair + 1) * WINDOW, :]
                odd = o[(2 * pair + 1) * WINDOW:(2 * pair + 2) * WINDOW, :]
                c0 = (kv * (SWA_GROUP // 2) + pair) * LANES
                ocat_scr[r0:r0 + WINDOW, c0:c0 + LANES] = jnp.where(lo, even, odd).astype(BF16)

    qm = proj[:, C_QM:C_XR] * SCALE
    o_mem = jnp.zeros((tq, Q_MEM), F32)
    for head in range(MEM_HEADS):
        hm = _head_mask(Q_MEM, head)
        s = _dot(jnp.where(hm, qm, 0.0).astype(BF16), mkt_scr[...])
        m = jnp.max(s, axis=-1, keepdims=True)
        e = jnp.exp(s - m)
        l = jnp.sum(e, axis=-1, keepdims=True)
        o = _dot(e.astype(BF16), mvb_scr[...]) / l
        o_mem = jnp.where(hm, o, o_mem)
    ocat_scr[:, Q_SWA:Q_SWA + Q_MEM] = o_mem.astype(BF16)

    xr = proj[:, C_XR:C_GR]
    xr_scr[8:8 + tq, :] = xr
    u = convb_ref[...]
    for jj in range(CONV_WIDTH):
        off = 8 - (CONV_WIDTH - 1) + jj
        u = u + xr_scr[off:off + tq, :] * convw_ref[jj:jj + 1, :]
    convlast_ref[0] = xr_scr[8 + tq - (CONV_WIDTH - 1):8 + tq, :]
    xr_scr[0:8, :] = xr_scr[tq:tq + 8, :]

    a, b = _rglru_gates(u, wa_ref, ba_ref, wx_ref, bx_ref, lam_ref)
    ri = lax.broadcasted_iota(jnp.int32, (tq, RNN_WIDTH), 0)
    step = 1
    while step < tq:
        live = ri >= step
        a_prev = jnp.where(live, pltpu.roll(a, step, 0), 1.0)
        b_prev = jnp.where(live, pltpu.roll(b, step, 0), 0.0)
        b = a * b_prev + b
        a = a * a_prev
        step *= 2
    h = a * h_scr[...] + b
    h_last = h[tq - 1:tq, :]
    h_scr[...] = h_last
    hlast_ref[0] = h_last
    ocat_scr[:, Q_SWA + Q_MEM:] = (h * jax.nn.gelu(proj[:, C_GR:])).astype(BF16)

    o = _dot(ocat_scr[...], wout_ref[...])
    y_ref[...] = x + _rms(o, gpost_ref[...])

    kd_scr[:, 0:WINDOW, :] = kd_scr[:, tq:tq + WINDOW, :]
    vd_scr[:, 0:WINDOW, :] = vd_scr[:, tq:tq + WINDOW, :]


def _mix_prompt(x, mk, mv, sinks, g_pre, w_in, w_out, g_post, conv_w, conv_b, wa, ba, wx, bx, lam,
                batch, seq, tq):
    nt = seq // tq
    row = pl.BlockSpec((tq, D_MODEL), lambda b, t: (b * nt + t, 0))
    per_b = lambda shape: pl.BlockSpec(shape, lambda b, t: (b,) + (0,) * (len(shape) - 1))
    return pl.pallas_call(
        _mix_prompt_kernel,
        grid=(batch, nt),
        in_specs=[pl.BlockSpec(memory_space=pltpu.SMEM),
                  row, per_b((N_MEM, Q_MEM)), per_b((N_MEM, Q_MEM)),
                  _const_spec((1, D_MODEL)), _const_spec((D_MODEL, D_IN)),
                  _const_spec((D_MIX, D_MODEL)), _const_spec((1, D_MODEL)),
                  _const_spec((CONV_WIDTH, RNN_WIDTH)), _const_spec((1, RNN_WIDTH)),
                  _const_spec((RNN_WIDTH, RNN_WIDTH)), _const_spec((1, RNN_WIDTH)),
                  _const_spec((RNN_WIDTH, RNN_WIDTH)), _const_spec((1, RNN_WIDTH)),
                  _const_spec((1, RNN_WIDTH))],
        out_specs=[row, per_b((WINDOW, KV_SWA)), per_b((WINDOW, KV_SWA)),
                   per_b((1, CONV_WIDTH - 1, RNN_WIDTH)), per_b((1, 1, RNN_WIDTH))],
        out_shape=[jax.ShapeDtypeStruct((batch * seq, D_MODEL), F32),
                   jax.ShapeDtypeStruct((batch * WINDOW, KV_SWA), F32),
                   jax.ShapeDtypeStruct((batch * WINDOW, KV_SWA), F32),
                   jax.ShapeDtypeStruct((batch, CONV_WIDTH - 1, RNN_WIDTH), F32),
                   jax.ShapeDtypeStruct((batch, 1, RNN_WIDTH), F32)],
        scratch_shapes=[pltpu.VMEM((SWA_KV_HEADS, WINDOW + tq, LANES), BF16),
                        pltpu.VMEM((SWA_KV_HEADS, WINDOW + tq, LANES), BF16),
                        pltpu.VMEM((Q_MEM, N_MEM), BF16),
                        pltpu.VMEM((N_MEM, Q_MEM), BF16),
                        pltpu.VMEM((8 + tq, RNN_WIDTH), F32),
                        pltpu.VMEM((1, RNN_WIDTH), F32),
                        pltpu.VMEM((tq, D_MIX), BF16)],
        compiler_params=_params(2),
        name="mix_prompt",
    )(sinks, x, mk, mv, g_pre, w_in, w_out, g_post, conv_w, conv_b, wa, ba, wx, bx, lam)


def _sample_pre_kernel(x_ref, gpre_ref, win_ref, conv_ref, h0_ref, convw_ref, convb_ref,
                       wa_ref, ba_ref, wx_ref, bx_ref, lam_ref,
                       qswa_ref, knew_ref, vnew_ref, qmem_ref, ornn_ref, newconv_ref, hnew_ref):
    xn = _rms(x_ref[...], gpre_ref[...]).astype(BF16)
    proj = _dot(xn, win_ref[...])
    lane = lax.broadcasted_iota(jnp.int32, (1, LANES), 1)
    lo = lane < HEAD_DIM
    for head in range(SWA_HEADS):
        kv = head // SWA_GROUP
        c0 = (head // 2) * LANES
        qt = proj[:, c0:c0 + LANES] * SCALE
        if head % 2 != kv:
            qt = pltpu.roll(qt, HEAD_DIM, 1)
        keep = lo if kv == 0 else jnp.logical_not(lo)
        qswa_ref[head] = jnp.where(keep, qt, 0.0)
    knew_ref[...] = proj[:, C_K:C_V]
    vnew_ref[...] = proj[:, C_V:C_QM]
    qm = proj[:, C_QM:C_XR] * SCALE
    for head in range(MEM_HEADS):
        qmem_ref[head] = jnp.where(_head_mask(Q_MEM, head), qm, 0.0)
    for head in range(MEM_HEADS, 8):
        qmem_ref[head] = jnp.zeros_like(qm)

    xr = proj[:, C_XR:C_GR]
    u = convb_ref[...] + xr * convw_ref[CONV_WIDTH - 1:CONV_WIDTH, :]
    for jj in range(CONV_WIDTH - 1):
        u = u + conv_ref[:, jj * RNN_WIDTH:(jj + 1) * RNN_WIDTH] * convw_ref[jj:jj + 1, :]
    newconv_ref[:, :(CONV_WIDTH - 2) * RNN_WIDTH] = conv_ref[:, RNN_WIDTH:]
    newconv_ref[:, (CONV_WIDTH - 2) * RNN_WIDTH:] = xr
    a, b = _rglru_gates(u, wa_ref, ba_ref, wx_ref, bx_ref, lam_ref)
    h = a * h0_ref[...] + b
    hnew_ref[...] = h
    ornn_ref[...] = h * jax.nn.gelu(proj[:, C_GR:])


def _sample_pre(x, g_pre, w_in, conv, h0, conv_w, conv_b, wa, ba, wx, bx, lam):
    n = x.shape[0]
    sds = lambda *shape: jax.ShapeDtypeStruct(shape, F32)
    return pl.pallas_call(
        _sample_pre_kernel,
        out_shape=[sds(SWA_HEADS, n, LANES), sds(n, KV_SWA), sds(n, KV_SWA), sds(8, n, Q_MEM),
                   sds(n, RNN_WIDTH), sds(n, (CONV_WIDTH - 1) * RNN_WIDTH), sds(n, RNN_WIDTH)],
        compiler_params=pltpu.CompilerParams(vmem_limit_bytes=VMEM_LIMIT),
        name="sample_pre",
    )(x, g_pre, w_in, conv, h0, conv_w, conv_b, wa, ba, wx, bx, lam)


def _sample_attn_kernel(sink_ref, q_ref, knew_ref, vnew_ref, qm_ref, ck_ref, cv_ref, cmk_ref, cmv_ref,
                        oswa_ref, omem_ref, nk_ref, nv_ref):
    sb = q_ref.shape[0]
    sink = sink_ref[:, 0:1]

    def body(b, carry):
        q = q_ref[b]
        kn = knew_ref[pl.ds(b, 1), :]
        vn = vnew_ref[pl.ds(b, 1), :]
        s = _dot_nt(q.astype(BF16), ck_ref[b].astype(BF16))
        s_new = jnp.sum(q * kn, axis=-1, keepdims=True)
        m = jnp.maximum(jnp.maximum(jnp.max(s, axis=-1, keepdims=True), s_new), sink)
        e = jnp.exp(s - m)
        e_new = jnp.exp(s_new - m)
        l = jnp.sum(e, axis=-1, keepdims=True) + e_new + jnp.exp(sink - m)
        o = _dot(e.astype(BF16), cv_ref[b].astype(BF16)) + e_new * vn
        oswa_ref[b] = o / l
        nk_ref[b, 0:WINDOW - 1, :] = ck_ref[b, 1:WINDOW, :]
        nk_ref[b, WINDOW - 1:WINDOW, :] = kn
        nv_ref[b, 0:WINDOW - 1, :] = cv_ref[b, 1:WINDOW, :]
        nv_ref[b, WINDOW - 1:WINDOW, :] = vn

        sm = _dot_nt(qm_ref[b].astype(BF16), cmk_ref[b].astype(BF16))
        mm = jnp.max(sm, axis=-1, keepdims=True)
        em = jnp.exp(sm - mm)
        lm = jnp.sum(em, axis=-1, keepdims=True)
        omem_ref[b] = _dot(em.astype(BF16), cmv_ref[b].astype(BF16)) / lm
        return carry

    lax.fori_loop(0, sb, body, 0)


def _sample_attn(sink, q, knew, vnew, qm, ck, cv, cmk, cmv, sb):
    n = q.shape[0]
    blk = lambda *shape: pl.BlockSpec((sb,) + shape, lambda i: (i,) + (0,) * len(shape))
    sds = lambda *shape: jax.ShapeDtypeStruct(shape, F32)
    return pl.pallas_call(
        _sample_attn_kernel,
        grid=(n // sb,),
        in_specs=[_const_spec((SWA_HEADS, LANES)),
                  blk(SWA_HEADS, LANES), blk(KV_SWA), blk(KV_SWA), blk(8, Q_MEM),
                  blk(WINDOW, KV_SWA), blk(WINDOW, KV_SWA), blk(N_MEM, Q_MEM), blk(N_MEM, Q_MEM)],
        out_specs=[blk(SWA_HEADS, LANES), blk(8, Q_MEM), blk(WINDOW, KV_SWA), blk(WINDOW, KV_SWA)],
        out_shape=[sds(n, SWA_HEADS, LANES), sds(n, 8, Q_MEM),
                   sds(n, WINDOW, KV_SWA), sds(n, WINDOW, KV_SWA)],
        compiler_params=_params(1),
        name="sample_attn",
    )(sink, q, knew, vnew, qm, ck, cv, cmk, cmv)


def _sample_post_kernel(x_ref, oswa_ref, omem_ref, ornn_ref, wout_ref, gpost_ref, y_ref):
    lane = lax.broadcasted_iota(jnp.int32, (1, LANES), 1)
    lo = lane < HEAD_DIM
    tiles = []
    for pair in range(SWA_HEADS // 2):
        kv = (2 * pair) // SWA_GROUP
        low = oswa_ref[2 * pair]
        high = oswa_ref[2 * pair + 1]
        if kv == 1:
            low = pltpu.roll(low, HEAD_DIM, 1)
        else:
            high = pltpu.roll(high, HEAD_DIM, 1)
        tiles.append(jnp.where(lo, low, high))
    o_mem = jnp.zeros(omem_ref.shape[1:], F32)
    for head in range(MEM_HEADS):
        o_mem = jnp.where(_head_mask(Q_MEM, head), omem_ref[head], o_mem)
    ocat = jnp.concatenate(tiles + [o_mem, ornn_ref[...]], axis=-1).astype(BF16)
    o = _dot(ocat, wout_ref[...])
    y_ref[...] = x_ref[...] + _rms(o, gpost_ref[...])


def _sample_post(x, oswa, omem, ornn, w_out, g_post):
    return pl.pallas_call(
        _sample_post_kernel,
        out_shape=jax.ShapeDtypeStruct(x.shape, F32),
        compiler_params=pltpu.CompilerParams(vmem_limit_bytes=VMEM_LIMIT),
        name="sample_post",
    )(x, oswa, omem, ornn, w_out, g_post)


def _block_diag(w):
    nb, d, _ = w.shape
    eye = jnp.eye(nb, dtype=w.dtype)
    return (eye[:, None, :, None] * w[:, :, None, :]).reshape(nb * d, nb * d)


def kernel(x_prompt, x_sample, mem_prompt, cache_swa_k, cache_swa_v, cache_mem_k, cache_mem_v, state_conv, state_rglru_h, ln_ffn1_pre, ln_ffn1_post, w_ffn1_in, w_ffn1_out, ln_mix_pre, ln_mix_post, w_in, w_out, swa_sinks, conv_w, conv_b, rg_wa, rg_ba, rg_wx, rg_bx, rg_lambda, ln_mem, w_mem_kv, ln_ffn2_pre, ln_ffn2_post, w_ffn2_in, w_ffn2_out):
    batch, seq, _ = x_prompt.shape
    n_dec = x_sample.shape[0]
    depth = w_in.shape[0]
    yp = x_prompt.reshape(batch * seq, D_MODEL)
    ys = x_sample.reshape(n_dec, D_MODEL)
    outs = [[] for _ in range(10)]
    for l in range(depth):
        row = lambda a: a[l].reshape(1, -1)
        w1i, w1o = w_ffn1_in[l].astype(BF16), w_ffn1_out[l].astype(BF16)
        w2i, w2o = w_ffn2_in[l].astype(BF16), w_ffn2_out[l].astype(BF16)
        wi, wo = w_in[l].astype(BF16), w_out[l].astype(BF16)
        wa = _block_diag(rg_wa[l]).astype(BF16)
        wx = _block_diag(rg_wx[l]).astype(BF16)
        rnn = (conv_w[l], row(conv_b), wa, row(rg_ba), wx, row(rg_bx), row(rg_lambda))

        mk, mv = _memkv(mem_prompt.reshape(batch * N_MEM, D_MODEL), row(ln_mem),
                        w_mem_kv[l].astype(BF16))
        yp = _ffn(yp, row(ln_ffn1_pre), row(ln_ffn1_post), w1i, w1o, FFN_TM)
        yp, kl, vl, cl, hl = _mix_prompt(yp, mk, mv, swa_sinks[l], row(ln_mix_pre), wi, wo,
                                         row(ln_mix_post), *rnn, batch, seq, MIX_TQ)
        yp = _ffn(yp, row(ln_ffn2_pre), row(ln_ffn2_post), w2i, w2o, FFN_TM)
        outs[0].append(kl.reshape(batch, WINDOW, SWA_KV_HEADS, HEAD_DIM))
        outs[1].append(vl.reshape(batch, WINDOW, SWA_KV_HEADS, HEAD_DIM))
        outs[2].append(mk.reshape(batch, N_MEM, MEM_HEADS, HEAD_DIM))
        outs[3].append(mv.reshape(batch, N_MEM, MEM_HEADS, HEAD_DIM))
        outs[4].append(cl)
        outs[5].append(hl.reshape(batch, RNN_WIDTH))

        ys = _ffn(ys, row(ln_ffn1_pre), row(ln_ffn1_post), w1i, w1o, n_dec)
        qswa, knew, vnew, qmem, ornn, newconv, hnew = _sample_pre(
            ys, row(ln_mix_pre), wi, state_conv[l].reshape(n_dec, -1), state_rglru_h[l], *rnn)
        sink = jnp.broadcast_to(swa_sinks[l][:, None], (SWA_HEADS, LANES))
        oswa, omem, nk, nv = _sample_attn(
            sink, qswa.transpose(1, 0, 2), knew, vnew, qmem.transpose(1, 0, 2),
            cache_swa_k[l].reshape(n_dec, WINDOW, KV_SWA), cache_swa_v[l].reshape(n_dec, WINDOW, KV_SWA),
            cache_mem_k[l].reshape(n_dec, N_MEM, Q_MEM), cache_mem_v[l].reshape(n_dec, N_MEM, Q_MEM),
            SAMPLE_SB)
        ys = _sample_post(ys, oswa.transpose(1, 0, 2), omem.transpose(1, 0, 2), ornn, wo,
                          row(ln_mix_post))
        ys = _ffn(ys, row(ln_ffn2_pre), row(ln_ffn2_post), w2i, w2o, n_dec)
        outs[6].append(nk.reshape(n_dec, WINDOW, SWA_KV_HEADS, HEAD_DIM))
        outs[7].append(nv.reshape(n_dec, WINDOW, SWA_KV_HEADS, HEAD_DIM))
        outs[8].append(newconv.reshape(n_dec, CONV_WIDTH - 1, RNN_WIDTH))
        outs[9].append(hnew)
    return (yp.reshape(batch, seq, D_MODEL), ys.reshape(n_dec, 1, D_MODEL),
            *[jnp.stack(o) for o in outs])
```

```python
import jax
import jax.numpy as jnp
from jax import lax
from jax.experimental import pallas as pl
from jax.experimental.pallas import tpu as pltpu

F32 = jnp.float32
BF16 = jnp.bfloat16

D_MODEL = 1024
HEAD_DIM = 64
SWA_HEADS = 8
SWA_KV_HEADS = 2
SWA_GROUP = SWA_HEADS // SWA_KV_HEADS
WINDOW = 128
N_MEM = 256
MEM_HEADS = 4
RNN_WIDTH = 256
RNN_BLOCKS = 4
RNN_BLOCK_DIM = RNN_WIDTH // RNN_BLOCKS
CONV_WIDTH = 4
RG_C = 8.0
D_FF = 2816
EPS = 1e-6
NEG = -1e30

Q_SWA = SWA_HEADS * HEAD_DIM
KV_SWA = SWA_KV_HEADS * HEAD_DIM
Q_MEM = MEM_HEADS * HEAD_DIM
D_IN = Q_SWA + 2 * KV_SWA + Q_MEM + 2 * RNN_WIDTH
D_MIX = Q_SWA + Q_MEM + RNN_WIDTH
C_K = Q_SWA
C_V = C_K + KV_SWA
C_QM = C_V + KV_SWA
C_XR = C_QM + Q_MEM
C_GR = C_XR + RNN_WIDTH
SCALE = HEAD_DIM ** -0.5
LOG2E = 1.4426950408889634

D_TOK = KV_SWA + 2 * RNN_WIDTH
D_FEAT = Q_SWA + KV_SWA + Q_MEM
R_V = Q_SWA
R_QM = Q_SWA + KV_SWA
D_ATT = Q_SWA + Q_MEM

LANES = 128
SUBLANES = 8
VMEM_LIMIT = 56 * 1024 * 1024

FFN_TM = 256
MIX_TQ = 512
SAMPLE_SB = 16


def _dot(a, b):
    return jnp.dot(a, b, preferred_element_type=F32)


def _dot_nt(a, b):
    return lax.dot_general(a, b, (((1,), (1,)), ((), ())), preferred_element_type=F32)


def _dot_tn(a, b):
    return lax.dot_general(a, b, (((0,), (0,)), ((), ())), preferred_element_type=F32)


def _rms(x, g):
    return x * lax.rsqrt(jnp.mean(x * x, axis=-1, keepdims=True) + EPS) * g


def _softplus(x):
    return jnp.maximum(x, 0.0) + jnp.log1p(jnp.exp(-jnp.abs(x)))


def _const_spec(shape):
    return pl.BlockSpec(shape, lambda *_: (0,) * len(shape), pipeline_mode=pl.Buffered(1))


def _params(n_grid):
    return pltpu.CompilerParams(dimension_semantics=("arbitrary",) * n_grid,
                                vmem_limit_bytes=VMEM_LIMIT)


def _ffn_kernel(x_ref, gpre_ref, gpost_ref, win_ref, wout_ref, o_ref):
    x = x_ref[...]
    xn = _rms(x, gpre_ref[...]).astype(BF16)
    gate = _dot(xn, win_ref[:, :D_FF])
    up = _dot(xn, win_ref[:, D_FF:])
    act = (gate * jax.nn.sigmoid(gate) * up).astype(BF16)
    y = _dot(act, wout_ref[...])
    o_ref[...] = x + 0.5 * _rms(y, gpost_ref[...])


def _ffn(x, g_pre, g_post, w_in, w_out, tm):
    m = x.shape[0]
    row = pl.BlockSpec((tm, D_MODEL), lambda i: (i, 0))
    return pl.pallas_call(
        _ffn_kernel,
        grid=(m // tm,),
        in_specs=[row, _const_spec((1, D_MODEL)), _const_spec((1, D_MODEL)),
                  _const_spec((D_MODEL, 2 * D_FF)), _const_spec((D_FF, D_MODEL))],
        out_specs=row,
        out_shape=jax.ShapeDtypeStruct((m, D_MODEL), F32),
        compiler_params=_params(1),
        name="ffn",
    )(x, g_pre, g_post, w_in, w_out)


def _memkv_kernel(m_ref, g_ref, w_ref, k_ref, v_ref):
    xn = _rms(m_ref[...], g_ref[...]).astype(BF16)
    kv = _dot(xn, w_ref[...])
    k_ref[...] = kv[:, :Q_MEM]
    v_ref[...] = kv[:, Q_MEM:]


def _memkv(mem, g, w):
    m = mem.shape[0]
    tm = 512
    out = pl.BlockSpec((tm, Q_MEM), lambda i: (i, 0))
    return pl.pallas_call(
        _memkv_kernel,
        grid=(m // tm,),
        in_specs=[pl.BlockSpec((tm, D_MODEL), lambda i: (i, 0)),
                  _const_spec((1, D_MODEL)), _const_spec((D_MODEL, 2 * Q_MEM))],
        out_specs=[out, out],
        out_shape=[jax.ShapeDtypeStruct((m, Q_MEM), F32)] * 2,
        compiler_params=_params(1),
        name="memkv",
    )(mem, g, w)


def _rglru_gates(u, wg_ref, bg_ref, lam_ref):
    gates = jax.nn.sigmoid(_dot(u.astype(BF16), wg_ref[...]) + bg_ref[...])
    r = gates[:, :RNN_WIDTH]
    i = gates[:, RNN_WIDTH:]
    log_a = -RG_C * r * _softplus(-lam_ref[...])
    a = jnp.exp(log_a)
    b = jnp.sqrt(1.0 - a * a) * (i * u)
    return a, b


def _head_mask(width, head):
    lane = lax.broadcasted_iota(jnp.int32, (1, width), 1)
    return (lane >= head * HEAD_DIM) & (lane < (head + 1) * HEAD_DIM)


def _mix_prompt_kernel(sinks_ref, x_ref, mk_ref, mv_ref, gpre_ref, wtok_ref, wfeat_ref, wout_ref,
                       gpost_ref, convw_ref, convb_ref, wg_ref, bg_ref, lam_ref,
                       y_ref, klast_ref, vlast_ref, convlast_ref, hlast_ref,
                       k_scr, vt_scr, mkb_scr, mvt_scr, bias_scr, xr_scr, xtail_scr, h_scr,
                       hs_scr, ps_scr, ot_scr):
    b_id = pl.program_id(0)
    t = pl.program_id(1)
    tq = x_ref.shape[0]
    nblk = tq // WINDOW
    chunk = tq // SUBLANES
    gw = SWA_GROUP * WINDOW

    @pl.when((b_id == 0) & (t == 0))
    def _():
        ki = lax.broadcasted_iota(jnp.int32, (2 * WINDOW, gw), 0)
        qi = lax.broadcasted_iota(jnp.int32, (2 * WINDOW, gw), 1) & (WINDOW - 1)
        band = (ki >= qi) & (ki <= qi + WINDOW)
        bias_scr[0] = jnp.where(band, 0.0, NEG)
        bias_scr[1] = jnp.where(band & (ki >= WINDOW), 0.0, NEG)

    @pl.when(t == 0)
    def _():
        k_scr[0:WINDOW, :] = jnp.zeros((WINDOW, KV_SWA), BF16)
        vt_scr[:, 0:WINDOW] = jnp.zeros((KV_SWA, WINDOW), BF16)
        xtail_scr[...] = jnp.zeros(xtail_scr.shape, F32)
        h_scr[...] = jnp.zeros((1, RNN_WIDTH), F32)
        mkb_scr[...] = mk_ref[...].astype(BF16)
        mvt_scr[...] = mv_ref[...].T.astype(BF16)

    x = x_ref[...]
    xn = _rms(x, gpre_ref[...]).astype(BF16)
    ptok = _dot(xn, wtok_ref[...])
    pfeat = _dot_nt(wfeat_ref[...], xn)

    k = ptok[:, :KV_SWA]
    vt = pfeat[R_V:R_QM, :]
    k_scr[WINDOW:, :] = k.astype(BF16)
    vt_scr[:, WINDOW:] = vt.astype(BF16)
    qt = (pfeat[:Q_SWA, :] * (SCALE * LOG2E)).astype(BF16)
    zero_head = jnp.zeros((HEAD_DIM, WINDOW), BF16)
    seg = lax.broadcasted_iota(jnp.int32, (1, gw), 1) // WINDOW
    for j in range(nblk):
        c0 = j * WINDOW
        bias = bias_scr[jnp.where(t == 0, 1, 0)] if j == 0 else bias_scr[0]
        for kv in range(SWA_KV_HEADS):
            cols = []
            sink = jnp.zeros((1, gw), F32)
            for g in range(SWA_GROUP):
                head = kv * SWA_GROUP + g
                qh = qt[head * HEAD_DIM:(head + 1) * HEAD_DIM, c0:c0 + WINDOW]
                cols.append(jnp.concatenate([qh, zero_head] if kv == 0 else [zero_head, qh], axis=0))
                sink = jnp.where(seg == g, sinks_ref[head] * LOG2E, sink)
            rhs = jnp.concatenate(cols, axis=1)
            s = _dot(k_scr[c0:c0 + 2 * WINDOW, :], rhs) + bias
            m = jnp.maximum(jnp.max(s, axis=0, keepdims=True), sink)
            e = jnp.exp2(s - m)
            l = jnp.sum(e, axis=0, keepdims=True) + jnp.exp2(sink - m)
            o = _dot(vt_scr[kv * HEAD_DIM:(kv + 1) * HEAD_DIM, c0:c0 + 2 * WINDOW], e.astype(BF16))
            o = o * (1.0 / l)
            for g in range(SWA_GROUP):
                head = kv * SWA_GROUP + g
                ot_scr[head * HEAD_DIM:(head + 1) * HEAD_DIM, c0:c0 + WINDOW] = (
                    o[:, g * WINDOW:(g + 1) * WINDOW].astype(BF16))

    qm = (pfeat[R_QM:, :] * (SCALE * LOG2E)).astype(BF16)
    cols = []
    for head in range(MEM_HEADS):
        parts = [qm[head * HEAD_DIM:(head + 1) * HEAD_DIM, :]]
        if head > 0:
            parts.insert(0, jnp.zeros((head * HEAD_DIM, tq), BF16))
        if head < MEM_HEADS - 1:
            parts.append(jnp.zeros(((MEM_HEADS - 1 - head) * HEAD_DIM, tq), BF16))
        cols.append(jnp.concatenate(parts, axis=0))
    s = _dot(mkb_scr[...], jnp.concatenate(cols, axis=1))
    m = jnp.max(s, axis=0, keepdims=True)
    e = jnp.exp2(s - m)
    inv_l = 1.0 / jnp.sum(e, axis=0, keepdims=True)
    eb = e.astype(BF16)
    for head in range(MEM_HEADS):
        o = _dot(mvt_scr[head * HEAD_DIM:(head + 1) * HEAD_DIM, :], eb[:, head * tq:(head + 1) * tq])
        ot_scr[Q_SWA + head * HEAD_DIM:Q_SWA + (head + 1) * HEAD_DIM, :] = (
            o * inv_l[:, head * tq:(head + 1) * tq]).astype(BF16)

    halves = RNN_WIDTH // LANES

    def put(ref, val):
        for hf in range(halves):
            ref[hf] = val[:, hf * LANES:(hf + 1) * LANES]

    def get(ref, rows=slice(None)):
        return jnp.concatenate([ref[hf, rows, :] for hf in range(halves)], axis=1)

    def get_perm(ref, r):
        return get(ref, pl.ds(r, SUBLANES, stride=chunk))

    def put_perm(ref, r, val):
        for hf in range(halves):
            ref[hf, pl.ds(r, SUBLANES, stride=chunk), :] = val[:, hf * LANES:(hf + 1) * LANES]

    put(xr_scr, ptok[:, KV_SWA:KV_SWA + RNN_WIDTH])
    xs = [get_perm(xr_scr, r) for r in range(chunk)]
    sub = lax.broadcasted_iota(jnp.int32, (SUBLANES, RNN_WIDTH), 0)
    taps = CONV_WIDTH - 1
    head_rows = [pltpu.roll(jnp.where(sub == SUBLANES - 1, xtail_scr[i], xs[chunk - taps + i]), 1, 0)
                 for i in range(taps)]
    for i in range(taps):
        xtail_scr[i] = xs[chunk - taps + i]
    xm = head_rows + xs
    w = [convw_ref[jj:jj + 1, :] for jj in range(CONV_WIDTH)]
    us = []
    for r in range(chunk):
        u = convb_ref[...] + xm[r] * w[0]
        for jj in range(1, CONV_WIDTH):
            u = u + xm[r + jj] * w[jj]
        us.append(u)
    a, b = _rglru_gates(jnp.concatenate(us, axis=0), wg_ref, bg_ref, lam_ref)
    ps = hs = None
    for r in range(chunk):
        ar = a[r * SUBLANES:(r + 1) * SUBLANES, :]
        br = b[r * SUBLANES:(r + 1) * SUBLANES, :]
        if r == 0:
            ps, hs = ar, br
        else:
            hs = ar * hs + br
            ps = ar * ps
        put_perm(hs_scr, r, hs)
        put_perm(ps_scr, r, ps)
    h_in = h_scr[...]
    h_in_rows = []
    for c in range(SUBLANES):
        h_in_rows.append(jnp.broadcast_to(h_in, (chunk, RNN_WIDTH)))
        h_in = ps[c:c + 1, :] * h_in + hs[c:c + 1, :]
    h_scr[...] = h_in
    h = get(hs_scr) + get(ps_scr) * jnp.concatenate(h_in_rows, axis=0)
    o_rnn = (h * jax.nn.gelu(ptok[:, KV_SWA + RNN_WIDTH:])).astype(BF16)

    o = _dot_tn(ot_scr[...], wout_ref[:D_ATT, :]) + _dot(o_rnn, wout_ref[D_ATT:, :])
    y_ref[...] = x + _rms(o, gpost_ref[...])

    k_scr[0:WINDOW, :] = k_scr[tq:tq + WINDOW, :]
    vt_scr[:, 0:WINDOW] = vt_scr[:, tq:tq + WINDOW]

    @pl.when(t == pl.num_programs(1) - 1)
    def _():
        klast_ref[...] = k[tq - WINDOW:, :]
        vlast_ref[...] = vt[:, tq - WINDOW:].T
        convlast_ref[0] = get(xr_scr, slice(tq - taps, tq))
        hlast_ref[0] = h_in


def _mix_prompt(x, mk, mv, sinks, g_pre, w_tok, w_feat, w_out, g_post, conv_w, conv_b, wg, bg, lam,
                batch, seq, tq):
    nt = seq // tq
    row = pl.BlockSpec((tq, D_MODEL), lambda b, t: (b * nt + t, 0))
    per_b = lambda shape: pl.BlockSpec(shape, lambda b, t: (b,) + (0,) * (len(shape) - 1))
    return pl.pallas_call(
        _mix_prompt_kernel,
        grid=(batch, nt),
        in_specs=[pl.BlockSpec(memory_space=pltpu.SMEM),
                  row, per_b((N_MEM, Q_MEM)), per_b((N_MEM, Q_MEM)),
                  _const_spec((1, D_MODEL)), _const_spec((D_MODEL, D_TOK)),
                  _const_spec((D_FEAT, D_MODEL)),
                  _const_spec((D_MIX, D_MODEL)), _const_spec((1, D_MODEL)),
                  _const_spec((CONV_WIDTH, RNN_WIDTH)), _const_spec((1, RNN_WIDTH)),
                  _const_spec((RNN_WIDTH, 2 * RNN_WIDTH)), _const_spec((1, 2 * RNN_WIDTH)),
                  _const_spec((1, RNN_WIDTH))],
        out_specs=[row, per_b((WINDOW, KV_SWA)), per_b((WINDOW, KV_SWA)),
                   per_b((1, CONV_WIDTH - 1, RNN_WIDTH)), per_b((1, 1, RNN_WIDTH))],
        out_shape=[jax.ShapeDtypeStruct((batch * seq, D_MODEL), F32),
                   jax.ShapeDtypeStruct((batch * WINDOW, KV_SWA), F32),
                   jax.ShapeDtypeStruct((batch * WINDOW, KV_SWA), F32),
                   jax.ShapeDtypeStruct((batch, CONV_WIDTH - 1, RNN_WIDTH), F32),
                   jax.ShapeDtypeStruct((batch, 1, RNN_WIDTH), F32)],
        scratch_shapes=[pltpu.VMEM((WINDOW + tq, KV_SWA), BF16),
                        pltpu.VMEM((KV_SWA, WINDOW + tq), BF16),
                        pltpu.VMEM((N_MEM, Q_MEM), BF16),
                        pltpu.VMEM((Q_MEM, N_MEM), BF16),
                        pltpu.VMEM((2, 2 * WINDOW, SWA_GROUP * WINDOW), F32),
                        pltpu.VMEM((RNN_WIDTH // LANES, tq, LANES), F32),
                        pltpu.VMEM((CONV_WIDTH - 1, SUBLANES, RNN_WIDTH), F32),
                        pltpu.VMEM((1, RNN_WIDTH), F32),
                        pltpu.VMEM((RNN_WIDTH // LANES, tq, LANES), F32),
                        pltpu.VMEM((RNN_WIDTH // LANES, tq, LANES), F32),
                        pltpu.VMEM((D_ATT, tq), BF16)],
        compiler_params=_params(2),
        name="mix_prompt",
    )(sinks, x, mk, mv, g_pre, w_tok, w_feat, w_out, g_post, conv_w, conv_b, wg, bg, lam)


def _sample_pre_kernel(x_ref, gpre_ref, win_ref, conv_ref, h0_ref, convw_ref, convb_ref,
                       wg_ref, bg_ref, lam_ref,
                       qswa_ref, knew_ref, vnew_ref, qmem_ref, ornn_ref, newconv_ref, hnew_ref):
    xn = _rms(x_ref[...], gpre_ref[...]).astype(BF16)
    proj = _dot(xn, win_ref[...])
    lane = lax.broadcasted_iota(jnp.int32, (1, LANES), 1)
    lo = lane < HEAD_DIM
    for head in range(SWA_HEADS):
        kv = head // SWA_GROUP
        c0 = (head // 2) * LANES
        qt = proj[:, c0:c0 + LANES] * SCALE
        if head % 2 != kv:
            qt = pltpu.roll(qt, HEAD_DIM, 1)
        keep = lo if kv == 0 else jnp.logical_not(lo)
        qswa_ref[head] = jnp.where(keep, qt, 0.0)
    knew_ref[...] = proj[:, C_K:C_V]
    vnew_ref[...] = proj[:, C_V:C_QM]
    qm = proj[:, C_QM:C_XR] * SCALE
    for head in range(MEM_HEADS):
        qmem_ref[head] = jnp.where(_head_mask(Q_MEM, head), qm, 0.0)
    for head in range(MEM_HEADS, 8):
        qmem_ref[head] = jnp.zeros_like(qm)

    xr = proj[:, C_XR:C_GR]
    u = convb_ref[...] + xr * convw_ref[CONV_WIDTH - 1:CONV_WIDTH, :]
    for jj in range(CONV_WIDTH - 1):
        u = u + conv_ref[:, jj * RNN_WIDTH:(jj + 1) * RNN_WIDTH] * convw_ref[jj:jj + 1, :]
    newconv_ref[:, :(CONV_WIDTH - 2) * RNN_WIDTH] = conv_ref[:, RNN_WIDTH:]
    newconv_ref[:, (CONV_WIDTH - 2) * RNN_WIDTH:] = xr
    a, b = _rglru_gates(u, wg_ref, bg_ref, lam_ref)
    h = a * h0_ref[...] + b
    hnew_ref[...] = h
    ornn_ref[...] = h * jax.nn.gelu(proj[:, C_GR:])


def _sample_pre(x, g_pre, w_in, conv, h0, conv_w, conv_b, wg, bg, lam):
    n = x.shape[0]
    sds = lambda *shape: jax.ShapeDtypeStruct(shape, F32)
    return pl.pallas_call(
        _sample_pre_kernel,
        out_shape=[sds(SWA_HEADS, n, LANES), sds(n, KV_SWA), sds(n, KV_SWA), sds(8, n, Q_MEM),
                   sds(n, RNN_WIDTH), sds(n, (CONV_WIDTH - 1) * RNN_WIDTH), sds(n, RNN_WIDTH)],
        compiler_params=pltpu.CompilerParams(vmem_limit_bytes=VMEM_LIMIT),
        name="sample_pre",
    )(x, g_pre, w_in, conv, h0, conv_w, conv_b, wg, bg, lam)


def _sample_attn_kernel(sink_ref, q_ref, knew_ref, vnew_ref, qm_ref, ck_ref, cv_ref, cmk_ref, cmv_ref,
                        oswa_ref, omem_ref, nk_ref, nv_ref):
    sb = q_ref.shape[0]
    sink = sink_ref[:, 0:1]

    def body(b, carry):
        q = q_ref[b]
        kn = knew_ref[pl.ds(b, 1), :]
        vn = vnew_ref[pl.ds(b, 1), :]
        s = _dot_nt(q.astype(BF16), ck_ref[b].astype(BF16))
        s_new = jnp.sum(q * kn, axis=-1, keepdims=True)
        m = jnp.maximum(jnp.maximum(jnp.max(s, axis=-1, keepdims=True), s_new), sink)
        e = jnp.exp(s - m)
        e_new = jnp.exp(s_new - m)
        l = jnp.sum(e, axis=-1, keepdims=True) + e_new + jnp.exp(sink - m)
        o = _dot(e.astype(BF16), cv_ref[b].astype(BF16)) + e_new * vn
        oswa_ref[b] = o / l
        nk_ref[b, 0:WINDOW - 1, :] = ck_ref[b, 1:WINDOW, :]
        nk_ref[b, WINDOW - 1:WINDOW, :] = kn
        nv_ref[b, 0:WINDOW - 1, :] = cv_ref[b, 1:WINDOW, :]
        nv_ref[b, WINDOW - 1:WINDOW, :] = vn

        sm = _dot_nt(qm_ref[b].astype(BF16), cmk_ref[b].astype(BF16))
        mm = jnp.max(sm, axis=-1, keepdims=True)
        em = jnp.exp(sm - mm)
        lm = jnp.sum(em, axis=-1, keepdims=True)
        omem_ref[b] = _dot(em.astype(BF16), cmv_ref[b].astype(BF16)) / lm
        return carry

    lax.fori_loop(0, sb, body, 0)


def _sample_attn(sink, q, knew, vnew, qm, ck, cv, cmk, cmv, sb):
    n = q.shape[0]
    blk = lambda *shape: pl.BlockSpec((sb,) + shape, lambda i: (i,) + (0,) * len(shape))
    sds = lambda *shape: jax.ShapeDtypeStruct(shape, F32)
    return pl.pallas_call(
        _sample_attn_kernel,
        grid=(n // sb,),
        in_specs=[_const_spec((SWA_HEADS, LANES)),
                  blk(SWA_HEADS, LANES), blk(KV_SWA), blk(KV_SWA), blk(8, Q_MEM),
                  blk(WINDOW, KV_SWA), blk(WINDOW, KV_SWA), blk(N_MEM, Q_MEM), blk(N_MEM, Q_MEM)],
        out_specs=[blk(SWA_HEADS, LANES), blk(8, Q_MEM), blk(WINDOW, KV_SWA), blk(WINDOW, KV_SWA)],
        out_shape=[sds(n, SWA_HEADS, LANES), sds(n, 8, Q_MEM),
                   sds(n, WINDOW, KV_SWA), sds(n, WINDOW, KV_SWA)],
        compiler_params=_params(1),
        name="sample_attn",
    )(sink, q, knew, vnew, qm, ck, cv, cmk, cmv)


def _sample_post_kernel(x_ref, oswa_ref, omem_ref, ornn_ref, wout_ref, gpost_ref, y_ref):
    lane = lax.broadcasted_iota(jnp.int32, (1, LANES), 1)
    lo = lane < HEAD_DIM
    tiles = []
    for pair in range(SWA_HEADS // 2):
        kv = (2 * pair) // SWA_GROUP
        low = oswa_ref[2 * pair]
        high = oswa_ref[2 * pair + 1]
        if kv == 1:
            low = pltpu.roll(low, HEAD_DIM, 1)
        else:
            high = pltpu.roll(high, HEAD_DIM, 1)
        tiles.append(jnp.where(lo, low, high))
    o_mem = jnp.zeros(omem_ref.shape[1:], F32)
    for head in range(MEM_HEADS):
        o_mem = jnp.where(_head_mask(Q_MEM, head), omem_ref[head], o_mem)
    ocat = jnp.concatenate(tiles + [o_mem, ornn_ref[...]], axis=-1).astype(BF16)
    o = _dot(ocat, wout_ref[...])
    y_ref[...] = x_ref[...] + _rms(o, gpost_ref[...])


def _sample_post(x, oswa, omem, ornn, w_out, g_post):
    return pl.pallas_call(
        _sample_post_kernel,
        out_shape=jax.ShapeDtypeStruct(x.shape, F32),
        compiler_params=pltpu.CompilerParams(vmem_limit_bytes=VMEM_LIMIT),
        name="sample_post",
    )(x, oswa, omem, ornn, w_out, g_post)


def _block_diag(w):
    nb, d, _ = w.shape
    eye = jnp.eye(nb, dtype=w.dtype)
    return (eye[:, None, :, None] * w[:, :, None, :]).reshape(nb * d, nb * d)


def kernel(x_prompt, x_sample, mem_prompt, cache_swa_k, cache_swa_v, cache_mem_k, cache_mem_v, state_conv, state_rglru_h, ln_ffn1_pre, ln_ffn1_post, w_ffn1_in, w_ffn1_out, ln_mix_pre, ln_mix_post, w_in, w_out, swa_sinks, conv_w, conv_b, rg_wa, rg_ba, rg_wx, rg_bx, rg_lambda, ln_mem, w_mem_kv, ln_ffn2_pre, ln_ffn2_post, w_ffn2_in, w_ffn2_out):
    batch, seq, _ = x_prompt.shape
    n_dec = x_sample.shape[0]
    depth = w_in.shape[0]
    yp = x_prompt.reshape(batch * seq, D_MODEL)
    ys = x_sample.reshape(n_dec, D_MODEL)
    outs = [[] for _ in range(10)]
    for l in range(depth):
        row = lambda a: a[l].reshape(1, -1)
        w1i, w1o = w_ffn1_in[l].astype(BF16), w_ffn1_out[l].astype(BF16)
        w2i, w2o = w_ffn2_in[l].astype(BF16), w_ffn2_out[l].astype(BF16)
        wi, wo = w_in[l].astype(BF16), w_out[l].astype(BF16)
        w_tok = jnp.concatenate([wi[:, C_K:C_V], wi[:, C_XR:]], axis=1)
        w_feat = jnp.concatenate([wi[:, :C_K], wi[:, C_V:C_XR]], axis=1).T
        wg = jnp.concatenate([_block_diag(rg_wa[l]), _block_diag(rg_wx[l])], axis=1).astype(BF16)
        bg = jnp.concatenate([row(rg_ba), row(rg_bx)], axis=1)
        rnn = (conv_w[l], row(conv_b), wg, bg, row(rg_lambda))

        mk, mv = _memkv(mem_prompt.reshape(batch * N_MEM, D_MODEL), row(ln_mem),
                        w_mem_kv[l].astype(BF16))
        yp = _ffn(yp, row(ln_ffn1_pre), row(ln_ffn1_post), w1i, w1o, FFN_TM)
        yp, kl, vl, cl, hl = _mix_prompt(yp, mk, mv, swa_sinks[l], row(ln_mix_pre), w_tok, w_feat, wo,
                                         row(ln_mix_post), *rnn, batch, seq, MIX_TQ)
        yp = _ffn(yp, row(ln_ffn2_pre), row(ln_ffn2_post), w2i, w2o, FFN_TM)
        outs[0].append(kl.reshape(batch, WINDOW, SWA_KV_HEADS, HEAD_DIM))
        outs[1].append(vl.reshape(batch, WINDOW, SWA_KV_HEADS, HEAD_DIM))
        outs[2].append(mk.reshape(batch, N_MEM, MEM_HEADS, HEAD_DIM))
        outs[3].append(mv.reshape(batch, N_MEM, MEM_HEADS, HEAD_DIM))
        outs[4].append(cl)
        outs[5].append(hl.reshape(batch, RNN_WIDTH))

        ys = _ffn(ys, row(ln_ffn1_pre), row(ln_ffn1_post), w1i, w1o, n_dec)
        qswa, knew, vnew, qmem, ornn, newconv, hnew = _sample_pre(
            ys, row(ln_mix_pre), wi, state_conv[l].reshape(n_dec, -1), state_rglru_h[l], *rnn)
        sink = jnp.broadcast_to(swa_sinks[l][:, None], (SWA_HEADS, LANES))
        oswa, omem, nk, nv = _sample_attn(
            sink, qswa.transpose(1, 0, 2), knew, vnew, qmem.transpose(1, 0, 2),
            cache_swa_k[l].reshape(n_dec, WINDOW, KV_SWA), cache_swa_v[l].reshape(n_dec, WINDOW, KV_SWA),
            cache_mem_k[l].reshape(n_dec, N_MEM, Q_MEM), cache_mem_v[l].reshape(n_dec, N_MEM, Q_MEM),
            SAMPLE_SB)
        ys = _sample_post(ys, oswa.transpose(1, 0, 2), omem.transpose(1, 0, 2), ornn, wo,
                          row(ln_mix_post))
        ys = _ffn(ys, row(ln_ffn2_pre), row(ln_ffn2_post), w2i, w2o, n_dec)
        outs[6].append(nk.reshape(n_dec, WINDOW, SWA_KV_HEADS, HEAD_DIM))
        outs[7].append(nv.reshape(n_dec, WINDOW, SWA_KV_HEADS, HEAD_DIM))
        outs[8].append(newconv.reshape(n_dec, CONV_WIDTH - 1, RNN_WIDTH))
        outs[9].append(hnew)
    return (yp.reshape(batch, seq, D_MODEL), ys.reshape(n_dec, 1, D_MODEL),
            *[jnp.stack(o) for o in outs])
```

```python
import jax
import jax.numpy as jnp
from jax import lax
from jax.experimental import pallas as pl
from jax.experimental.pallas import tpu as pltpu

F32 = jnp.float32
BF16 = jnp.bfloat16

D_MODEL = 1024
HEAD_DIM = 64
SWA_HEADS = 8
SWA_KV_HEADS = 2
SWA_GROUP = SWA_HEADS // SWA_KV_HEADS
WINDOW = 128
N_MEM = 256
MEM_HEADS = 4
RNN_WIDTH = 256
RNN_BLOCKS = 4
RNN_BLOCK_DIM = RNN_WIDTH // RNN_BLOCKS
CONV_WIDTH = 4
RG_C = 8.0
D_FF = 2816
EPS = 1e-6
NEG = -1e30

Q_SWA = SWA_HEADS * HEAD_DIM
KV_SWA = SWA_KV_HEADS * HEAD_DIM
Q_MEM = MEM_HEADS * HEAD_DIM
D_IN = Q_SWA + 2 * KV_SWA + Q_MEM + 2 * RNN_WIDTH
D_MIX = Q_SWA + Q_MEM + RNN_WIDTH
C_K = Q_SWA
C_V = C_K + KV_SWA
C_QM = C_V + KV_SWA
C_XR = C_QM + Q_MEM
C_GR = C_XR + RNN_WIDTH
SCALE = HEAD_DIM ** -0.5
LOG2E = 1.4426950408889634

D_TOK = KV_SWA + 2 * RNN_WIDTH
D_FEAT = Q_SWA + KV_SWA + Q_MEM
R_V = Q_SWA
R_QM = Q_SWA + KV_SWA
D_ATT = Q_SWA + Q_MEM

LANES = 128
SUBLANES = 8
VMEM_LIMIT = 56 * 1024 * 1024

FFN_TM = 512
FFN_SUB = 256
MIX_TQ = 512
SAMPLE_SB = 8


def _dot(a, b):
    return jnp.dot(a, b, preferred_element_type=F32)


def _dot_nt(a, b):
    return lax.dot_general(a, b, (((1,), (1,)), ((), ())), preferred_element_type=F32)


def _dot_tn(a, b):
    return lax.dot_general(a, b, (((0,), (0,)), ((), ())), preferred_element_type=F32)


def _rms(x, g):
    return x * lax.rsqrt(jnp.mean(x * x, axis=-1, keepdims=True) + EPS) * g


def _softplus(x):
    return jnp.maximum(x, 0.0) + jnp.log1p(jnp.exp(-jnp.abs(x)))


def _const_spec(shape):
    return pl.BlockSpec(shape, lambda *_: (0,) * len(shape), pipeline_mode=pl.Buffered(1))


def _params(n_grid):
    return pltpu.CompilerParams(dimension_semantics=("arbitrary",) * n_grid,
                                vmem_limit_bytes=VMEM_LIMIT)


def _ffn_kernel(x_ref, gpre_ref, gpost_ref, win_ref, wout_ref, o_ref):
    tm = x_ref.shape[0]
    sub = min(tm, FFN_SUB)
    for r0 in range(0, tm, sub):
        x = x_ref[r0:r0 + sub, :]
        xn = _rms(x, gpre_ref[...]).astype(BF16)
        gate = _dot(xn, win_ref[:, :D_FF])
        up = _dot(xn, win_ref[:, D_FF:])
        act = (gate * jax.nn.sigmoid(gate) * up).astype(BF16)
        y = _dot(act, wout_ref[...])
        o_ref[r0:r0 + sub, :] = x + 0.5 * _rms(y, gpost_ref[...])


def _ffn(x, g_pre, g_post, w_in, w_out, tm):
    m = x.shape[0]
    row = pl.BlockSpec((tm, D_MODEL), lambda i: (i, 0))
    return pl.pallas_call(
        _ffn_kernel,
        grid=(m // tm,),
        in_specs=[row, _const_spec((1, D_MODEL)), _const_spec((1, D_MODEL)),
                  _const_spec((D_MODEL, 2 * D_FF)), _const_spec((D_FF, D_MODEL))],
        out_specs=row,
        out_shape=jax.ShapeDtypeStruct((m, D_MODEL), F32),
        compiler_params=_params(1),
        name="ffn",
    )(x, g_pre, g_post, w_in, w_out)


def _memkv_kernel(m_ref, g_ref, w_ref, k_ref, v_ref):
    xn = _rms(m_ref[...], g_ref[...]).astype(BF16)
    kv = _dot(xn, w_ref[...])
    k_ref[...] = kv[:, :Q_MEM]
    v_ref[...] = kv[:, Q_MEM:]


def _memkv(mem, g, w):
    m = mem.shape[0]
    tm = 512
    out = pl.BlockSpec((tm, Q_MEM), lambda i: (i, 0))
    return pl.pallas_call(
        _memkv_kernel,
        grid=(m // tm,),
        in_specs=[pl.BlockSpec((tm, D_MODEL), lambda i: (i, 0)),
                  _const_spec((1, D_MODEL)), _const_spec((D_MODEL, 2 * Q_MEM))],
        out_specs=[out, out],
        out_shape=[jax.ShapeDtypeStruct((m, Q_MEM), F32)] * 2,
        compiler_params=_params(1),
        name="memkv",
    )(mem, g, w)


def _rglru_gates(u, wg_ref, bg_ref, lam_ref):
    gates = jax.nn.sigmoid(_dot(u.astype(BF16), wg_ref[...]) + bg_ref[...])
    r = gates[:, :RNN_WIDTH]
    i = gates[:, RNN_WIDTH:]
    log_a = -RG_C * r * _softplus(-lam_ref[...])
    a = jnp.exp(log_a)
    b = jnp.sqrt(1.0 - a * a) * (i * u)
    return a, b


def _head_mask(width, head):
    lane = lax.broadcasted_iota(jnp.int32, (1, width), 1)
    return (lane >= head * HEAD_DIM) & (lane < (head + 1) * HEAD_DIM)


def _mix_prompt_kernel(sinks_ref, x_ref, mk_ref, mv_ref, gpre_ref, wtok_ref, wfeat_ref, wout_ref,
                       gpost_ref, convw_ref, convb_ref, wg_ref, bg_ref, lam_ref,
                       y_ref, klast_ref, vlast_ref, convlast_ref, hlast_ref,
                       k_scr, vt_scr, mkb_scr, mvt_scr, bias_scr, xr_scr, xtail_scr, h_scr,
                       hs_scr, ps_scr, ot_scr):
    b_id = pl.program_id(0)
    t = pl.program_id(1)
    tq = x_ref.shape[0]
    nblk = tq // WINDOW
    chunk = tq // SUBLANES
    gw = SWA_GROUP * WINDOW

    @pl.when((b_id == 0) & (t == 0))
    def _():
        ki = lax.broadcasted_iota(jnp.int32, (2 * WINDOW, gw), 0)
        qi = lax.broadcasted_iota(jnp.int32, (2 * WINDOW, gw), 1) & (WINDOW - 1)
        band = (ki >= qi) & (ki <= qi + WINDOW)
        bias_scr[0] = jnp.where(band, 0.0, NEG)
        bias_scr[1] = jnp.where(band & (ki >= WINDOW), 0.0, NEG)

    @pl.when(t == 0)
    def _():
        k_scr[0:WINDOW, :] = jnp.zeros((WINDOW, KV_SWA), BF16)
        vt_scr[:, 0:WINDOW] = jnp.zeros((KV_SWA, WINDOW), BF16)
        xtail_scr[...] = jnp.zeros(xtail_scr.shape, F32)
        h_scr[...] = jnp.zeros((1, RNN_WIDTH), F32)
        mkb_scr[...] = mk_ref[...].astype(BF16)
        mvt_scr[...] = mv_ref[...].T.astype(BF16)

    x = x_ref[...]
    xn = _rms(x, gpre_ref[...]).astype(BF16)
    ptok = _dot(xn, wtok_ref[...])
    pfeat = _dot_nt(wfeat_ref[...], xn)

    k = ptok[:, :KV_SWA]
    vt = pfeat[R_V:R_QM, :]
    k_scr[WINDOW:, :] = k.astype(BF16)
    vt_scr[:, WINDOW:] = vt.astype(BF16)
    qt = (pfeat[:Q_SWA, :] * (SCALE * LOG2E)).astype(BF16)
    zero_head = jnp.zeros((HEAD_DIM, WINDOW), BF16)
    seg = lax.broadcasted_iota(jnp.int32, (1, gw), 1) // WINDOW
    for j in range(nblk):
        c0 = j * WINDOW
        bias = bias_scr[jnp.where(t == 0, 1, 0)] if j == 0 else bias_scr[0]
        for kv in range(SWA_KV_HEADS):
            cols = []
            sink = jnp.zeros((1, gw), F32)
            for g in range(SWA_GROUP):
                head = kv * SWA_GROUP + g
                qh = qt[head * HEAD_DIM:(head + 1) * HEAD_DIM, c0:c0 + WINDOW]
                cols.append(jnp.concatenate([qh, zero_head] if kv == 0 else [zero_head, qh], axis=0))
                sink = jnp.where(seg == g, sinks_ref[head] * LOG2E, sink)
            rhs = jnp.concatenate(cols, axis=1)
            s = _dot(k_scr[c0:c0 + 2 * WINDOW, :], rhs) + bias
            m = jnp.maximum(jnp.max(s, axis=0, keepdims=True), sink)
            e = jnp.exp2(s - m)
            l = jnp.sum(e, axis=0, keepdims=True) + jnp.exp2(sink - m)
            o = _dot(vt_scr[kv * HEAD_DIM:(kv + 1) * HEAD_DIM, c0:c0 + 2 * WINDOW], e.astype(BF16))
            o = o * (1.0 / l)
            for g in range(SWA_GROUP):
                head = kv * SWA_GROUP + g
                ot_scr[head * HEAD_DIM:(head + 1) * HEAD_DIM, c0:c0 + WINDOW] = (
                    o[:, g * WINDOW:(g + 1) * WINDOW].astype(BF16))

    qm = (pfeat[R_QM:, :] * (SCALE * LOG2E)).astype(BF16)
    cols = []
    for head in range(MEM_HEADS):
        parts = [qm[head * HEAD_DIM:(head + 1) * HEAD_DIM, :]]
        if head > 0:
            parts.insert(0, jnp.zeros((head * HEAD_DIM, tq), BF16))
        if head < MEM_HEADS - 1:
            parts.append(jnp.zeros(((MEM_HEADS - 1 - head) * HEAD_DIM, tq), BF16))
        cols.append(jnp.concatenate(parts, axis=0))
    s = _dot(mkb_scr[...], jnp.concatenate(cols, axis=1))
    m = jnp.max(s, axis=0, keepdims=True)
    e = jnp.exp2(s - m)
    inv_l = 1.0 / jnp.sum(e, axis=0, keepdims=True)
    eb = e.astype(BF16)
    for head in range(MEM_HEADS):
        o = _dot(mvt_scr[head * HEAD_DIM:(head + 1) * HEAD_DIM, :], eb[:, head * tq:(head + 1) * tq])
        ot_scr[Q_SWA + head * HEAD_DIM:Q_SWA + (head + 1) * HEAD_DIM, :] = (
            o * inv_l[:, head * tq:(head + 1) * tq]).astype(BF16)

    halves = RNN_WIDTH // LANES

    def put(ref, val):
        for hf in range(halves):
            ref[hf] = val[:, hf * LANES:(hf + 1) * LANES]

    def get(ref, rows=slice(None)):
        return jnp.concatenate([ref[hf, rows, :] for hf in range(halves)], axis=1)

    def get_perm(ref, r):
        return get(ref, pl.ds(r, SUBLANES, stride=chunk))

    def put_perm(ref, r, val):
        for hf in range(halves):
            ref[hf, pl.ds(r, SUBLANES, stride=chunk), :] = val[:, hf * LANES:(hf + 1) * LANES]

    put(xr_scr, ptok[:, KV_SWA:KV_SWA + RNN_WIDTH])
    xs = [get_perm(xr_scr, r) for r in range(chunk)]
    sub = lax.broadcasted_iota(jnp.int32, (SUBLANES, RNN_WIDTH), 0)
    taps = CONV_WIDTH - 1
    head_rows = [pltpu.roll(jnp.where(sub == SUBLANES - 1, xtail_scr[i], xs[chunk - taps + i]), 1, 0)
                 for i in range(taps)]
    for i in range(taps):
        xtail_scr[i] = xs[chunk - taps + i]
    xm = head_rows + xs
    w = [convw_ref[jj:jj + 1, :] for jj in range(CONV_WIDTH)]
    us = []
    for r in range(chunk):
        u = convb_ref[...] + xm[r] * w[0]
        for jj in range(1, CONV_WIDTH):
            u = u + xm[r + jj] * w[jj]
        us.append(u)
    a, b = _rglru_gates(jnp.concatenate(us, axis=0), wg_ref, bg_ref, lam_ref)
    ps = hs = None
    for r in range(chunk):
        ar = a[r * SUBLANES:(r + 1) * SUBLANES, :]
        br = b[r * SUBLANES:(r + 1) * SUBLANES, :]
        if r == 0:
            ps, hs = ar, br
        else:
            hs = ar * hs + br
            ps = ar * ps
        put_perm(hs_scr, r, hs)
        put_perm(ps_scr, r, ps)
    h_in = h_scr[...]
    h_in_rows = []
    for c in range(SUBLANES):
        h_in_rows.append(jnp.broadcast_to(h_in, (chunk, RNN_WIDTH)))
        h_in = ps[c:c + 1, :] * h_in + hs[c:c + 1, :]
    h_scr[...] = h_in
    h = get(hs_scr) + get(ps_scr) * jnp.concatenate(h_in_rows, axis=0)
    o_rnn = (h * jax.nn.gelu(ptok[:, KV_SWA + RNN_WIDTH:])).astype(BF16)

    o = _dot_tn(ot_scr[...], wout_ref[:D_ATT, :]) + _dot(o_rnn, wout_ref[D_ATT:, :])
    y_ref[...] = x + _rms(o, gpost_ref[...])

    k_scr[0:WINDOW, :] = k_scr[tq:tq + WINDOW, :]
    vt_scr[:, 0:WINDOW] = vt_scr[:, tq:tq + WINDOW]

    @pl.when(t == pl.num_programs(1) - 1)
    def _():
        klast_ref[...] = k[tq - WINDOW:, :]
        vlast_ref[...] = vt[:, tq - WINDOW:].T
        convlast_ref[0] = get(xr_scr, slice(tq - taps, tq))
        hlast_ref[0] = h_in


def _mix_prompt(x, mk, mv, sinks, g_pre, w_tok, w_feat, w_out, g_post, conv_w, conv_b, wg, bg, lam,
                batch, seq, tq):
    nt = seq // tq
    row = pl.BlockSpec((tq, D_MODEL), lambda b, t: (b * nt + t, 0))
    per_b = lambda shape: pl.BlockSpec(shape, lambda b, t: (b,) + (0,) * (len(shape) - 1))
    return pl.pallas_call(
        _mix_prompt_kernel,
        grid=(batch, nt),
        in_specs=[pl.BlockSpec(memory_space=pltpu.SMEM),
                  row, per_b((N_MEM, Q_MEM)), per_b((N_MEM, Q_MEM)),
                  _const_spec((1, D_MODEL)), _const_spec((D_MODEL, D_TOK)),
                  _const_spec((D_FEAT, D_MODEL)),
                  _const_spec((D_MIX, D_MODEL)), _const_spec((1, D_MODEL)),
                  _const_spec((CONV_WIDTH, RNN_WIDTH)), _const_spec((1, RNN_WIDTH)),
                  _const_spec((RNN_WIDTH, 2 * RNN_WIDTH)), _const_spec((1, 2 * RNN_WIDTH)),
                  _const_spec((1, RNN_WIDTH))],
        out_specs=[row, per_b((WINDOW, KV_SWA)), per_b((WINDOW, KV_SWA)),
                   per_b((1, CONV_WIDTH - 1, RNN_WIDTH)), per_b((1, 1, RNN_WIDTH))],
        out_shape=[jax.ShapeDtypeStruct((batch * seq, D_MODEL), F32),
                   jax.ShapeDtypeStruct((batch * WINDOW, KV_SWA), F32),
                   jax.ShapeDtypeStruct((batch * WINDOW, KV_SWA), F32),
                   jax.ShapeDtypeStruct((batch, CONV_WIDTH - 1, RNN_WIDTH), F32),
                   jax.ShapeDtypeStruct((batch, 1, RNN_WIDTH), F32)],
        scratch_shapes=[pltpu.VMEM((WINDOW + tq, KV_SWA), BF16),
                        pltpu.VMEM((KV_SWA, WINDOW + tq), BF16),
                        pltpu.VMEM((N_MEM, Q_MEM), BF16),
                        pltpu.VMEM((Q_MEM, N_MEM), BF16),
                        pltpu.VMEM((2, 2 * WINDOW, SWA_GROUP * WINDOW), F32),
                        pltpu.VMEM((RNN_WIDTH // LANES, tq, LANES), F32),
                        pltpu.VMEM((CONV_WIDTH - 1, SUBLANES, RNN_WIDTH), F32),
                        pltpu.VMEM((1, RNN_WIDTH), F32),
                        pltpu.VMEM((RNN_WIDTH // LANES, tq, LANES), F32),
                        pltpu.VMEM((RNN_WIDTH // LANES, tq, LANES), F32),
                        pltpu.VMEM((D_ATT, tq), BF16)],
        compiler_params=_params(2),
        name="mix_prompt",
    )(sinks, x, mk, mv, g_pre, w_tok, w_feat, w_out, g_post, conv_w, conv_b, wg, bg, lam)


def _sample_pre_kernel(x_ref, gpre_ref, win_ref, conv_ref, h0_ref, convw_ref, convb_ref,
                       wg_ref, bg_ref, lam_ref,
                       qswa_ref, knew_ref, vnew_ref, qmem_ref, ornn_ref, newconv_ref, hnew_ref):
    xn = _rms(x_ref[...], gpre_ref[...]).astype(BF16)
    proj = _dot(xn, win_ref[...])
    lane = lax.broadcasted_iota(jnp.int32, (1, LANES), 1)
    lo = lane < HEAD_DIM
    for head in range(SWA_HEADS):
        kv = head // SWA_GROUP
        c0 = (head // 2) * LANES
        qt = proj[:, c0:c0 + LANES] * SCALE
        if head % 2 != kv:
            qt = pltpu.roll(qt, HEAD_DIM, 1)
        keep = lo if kv == 0 else jnp.logical_not(lo)
        qswa_ref[head] = jnp.where(keep, qt, 0.0)
    knew_ref[...] = proj[:, C_K:C_V]
    vnew_ref[...] = proj[:, C_V:C_QM]
    qm = proj[:, C_QM:C_XR] * SCALE
    for head in range(MEM_HEADS):
        qmem_ref[head] = jnp.where(_head_mask(Q_MEM, head), qm, 0.0)
    for head in range(MEM_HEADS, 8):
        qmem_ref[head] = jnp.zeros_like(qm)

    xr = proj[:, C_XR:C_GR]
    u = convb_ref[...] + xr * convw_ref[CONV_WIDTH - 1:CONV_WIDTH, :]
    for jj in range(CONV_WIDTH - 1):
        u = u + conv_ref[:, jj * RNN_WIDTH:(jj + 1) * RNN_WIDTH] * convw_ref[jj:jj + 1, :]
    newconv_ref[:, :(CONV_WIDTH - 2) * RNN_WIDTH] = conv_ref[:, RNN_WIDTH:]
    newconv_ref[:, (CONV_WIDTH - 2) * RNN_WIDTH:] = xr
    a, b = _rglru_gates(u, wg_ref, bg_ref, lam_ref)
    h = a * h0_ref[...] + b
    hnew_ref[...] = h
    ornn_ref[...] = h * jax.nn.gelu(proj[:, C_GR:])


def _sample_pre(x, g_pre, w_in, conv, h0, conv_w, conv_b, wg, bg, lam):
    n = x.shape[0]
    sds = lambda *shape: jax.ShapeDtypeStruct(shape, F32)
    return pl.pallas_call(
        _sample_pre_kernel,
        out_shape=[sds(SWA_HEADS, n, LANES), sds(n, KV_SWA), sds(n, KV_SWA), sds(8, n, Q_MEM),
                   sds(n, RNN_WIDTH), sds(n, (CONV_WIDTH - 1) * RNN_WIDTH), sds(n, RNN_WIDTH)],
        compiler_params=pltpu.CompilerParams(vmem_limit_bytes=VMEM_LIMIT),
        name="sample_pre",
    )(x, g_pre, w_in, conv, h0, conv_w, conv_b, wg, bg, lam)


def _sample_attn_kernel(sink_ref, q_ref, knew_ref, vnew_ref, qm_ref, ck_ref, cv_ref, cmk_ref, cmv_ref,
                        oswa_ref, omem_ref, nk_ref, nv_ref):
    sb, nh = q_ref.shape[0], q_ref.shape[1]
    rows = lambda b: slice(b * nh, (b + 1) * nh)
    stack = lambda f: jnp.concatenate([f(b) for b in range(sb)], axis=0)
    per_head = lambda ref, b: jnp.broadcast_to(ref[b:b + 1, :], (nh, ref.shape[1]))
    q = stack(lambda b: q_ref[b])
    sink = stack(lambda b: sink_ref[:, 0:1])
    s = stack(lambda b: _dot_nt(q_ref[b].astype(BF16), ck_ref[b].astype(BF16)))
    s_new = jnp.sum(q * stack(lambda b: per_head(knew_ref, b)), axis=-1, keepdims=True)
    m = jnp.maximum(jnp.maximum(jnp.max(s, axis=-1, keepdims=True), s_new), sink)
    e = jnp.exp(s - m)
    e_new = jnp.exp(s_new - m)
    l = jnp.sum(e, axis=-1, keepdims=True) + e_new + jnp.exp(sink - m)
    o = stack(lambda b: _dot(e[rows(b)].astype(BF16), cv_ref[b].astype(BF16)))
    o = (o + e_new * stack(lambda b: per_head(vnew_ref, b))) / l
    for b in range(sb):
        oswa_ref[b] = o[rows(b)]
        nk_ref[b, 0:WINDOW - 1, :] = ck_ref[b, 1:WINDOW, :]
        nk_ref[b, WINDOW - 1:WINDOW, :] = knew_ref[b:b + 1, :]
        nv_ref[b, 0:WINDOW - 1, :] = cv_ref[b, 1:WINDOW, :]
        nv_ref[b, WINDOW - 1:WINDOW, :] = vnew_ref[b:b + 1, :]

    sm = stack(lambda b: _dot_nt(qm_ref[b].astype(BF16), cmk_ref[b].astype(BF16)))
    em = jnp.exp(sm - jnp.max(sm, axis=-1, keepdims=True))
    lm = jnp.sum(em, axis=-1, keepdims=True)
    om = stack(lambda b: _dot(em[rows(b)].astype(BF16), cmv_ref[b].astype(BF16))) / lm
    for b in range(sb):
        omem_ref[b] = om[rows(b)]


def _sample_attn(sink, q, knew, vnew, qm, ck, cv, cmk, cmv, sb):
    n = q.shape[0]
    blk = lambda *shape: pl.BlockSpec((sb,) + shape, lambda i: (i,) + (0,) * len(shape))
    sds = lambda *shape: jax.ShapeDtypeStruct(shape, F32)
    return pl.pallas_call(
        _sample_attn_kernel,
        grid=(n // sb,),
        in_specs=[_const_spec((SWA_HEADS, LANES)),
                  blk(SWA_HEADS, LANES), blk(KV_SWA), blk(KV_SWA), blk(8, Q_MEM),
                  blk(WINDOW, KV_SWA), blk(WINDOW, KV_SWA), blk(N_MEM, Q_MEM), blk(N_MEM, Q_MEM)],
        out_specs=[blk(SWA_HEADS, LANES), blk(8, Q_MEM), blk(WINDOW, KV_SWA), blk(WINDOW, KV_SWA)],
        out_shape=[sds(n, SWA_HEADS, LANES), sds(n, 8, Q_MEM),
                   sds(n, WINDOW, KV_SWA), sds(n, WINDOW, KV_SWA)],
        compiler_params=_params(1),
        name="sample_attn",
    )(sink, q, knew, vnew, qm, ck, cv, cmk, cmv)


def _sample_post_kernel(x_ref, oswa_ref, omem_ref, ornn_ref, wout_ref, gpost_ref, y_ref):
    lane = lax.broadcasted_iota(jnp.int32, (1, LANES), 1)
    lo = lane < HEAD_DIM
    tiles = []
    for pair in range(SWA_HEADS // 2):
        kv = (2 * pair) // SWA_GROUP
        low = oswa_ref[2 * pair]
        high = oswa_ref[2 * pair + 1]
        if kv == 1:
            low = pltpu.roll(low, HEAD_DIM, 1)
        else:
            high = pltpu.roll(high, HEAD_DIM, 1)
        tiles.append(jnp.where(lo, low, high))
    o_mem = jnp.zeros(omem_ref.shape[1:], F32)
    for head in range(MEM_HEADS):
        o_mem = jnp.where(_head_mask(Q_MEM, head), omem_ref[head], o_mem)
    ocat = jnp.concatenate(tiles + [o_mem, ornn_ref[...]], axis=-1).astype(BF16)
    o = _dot(ocat, wout_ref[...])
    y_ref[...] = x_ref[...] + _rms(o, gpost_ref[...])


def _sample_post(x, oswa, omem, ornn, w_out, g_post):
    return pl.pallas_call(
        _sample_post_kernel,
        out_shape=jax.ShapeDtypeStruct(x.shape, F32),
        compiler_params=pltpu.CompilerParams(vmem_limit_bytes=VMEM_LIMIT),
        name="sample_post",
    )(x, oswa, omem, ornn, w_out, g_post)


def _block_diag(w):
    nb, d, _ = w.shape
    eye = jnp.eye(nb, dtype=w.dtype)
    return (eye[:, None, :, None] * w[:, :, None, :]).reshape(nb * d, nb * d)


def kernel(x_prompt, x_sample, mem_prompt, cache_swa_k, cache_swa_v, cache_mem_k, cache_mem_v, state_conv, state_rglru_h, ln_ffn1_pre, ln_ffn1_post, w_ffn1_in, w_ffn1_out, ln_mix_pre, ln_mix_post, w_in, w_out, swa_sinks, conv_w, conv_b, rg_wa, rg_ba, rg_wx, rg_bx, rg_lambda, ln_mem, w_mem_kv, ln_ffn2_pre, ln_ffn2_post, w_ffn2_in, w_ffn2_out):
    batch, seq, _ = x_prompt.shape
    n_dec = x_sample.shape[0]
    depth = w_in.shape[0]
    yp = x_prompt.reshape(batch * seq, D_MODEL)
    ys = x_sample.reshape(n_dec, D_MODEL)
    outs = [[] for _ in range(10)]
    for l in range(depth):
        row = lambda a: a[l].reshape(1, -1)
        w1i, w1o = w_ffn1_in[l].astype(BF16), w_ffn1_out[l].astype(BF16)
        w2i, w2o = w_ffn2_in[l].astype(BF16), w_ffn2_out[l].astype(BF16)
        wi, wo = w_in[l].astype(BF16), w_out[l].astype(BF16)
        w_tok = jnp.concatenate([wi[:, C_K:C_V], wi[:, C_XR:]], axis=1)
        w_feat = jnp.concatenate([wi[:, :C_K], wi[:, C_V:C_XR]], axis=1).T
        wg = jnp.concatenate([_block_diag(rg_wa[l]), _block_diag(rg_wx[l])], axis=1).astype(BF16)
        bg = jnp.concatenate([row(rg_ba), row(rg_bx)], axis=1)
        rnn = (conv_w[l], row(conv_b), wg, bg, row(rg_lambda))

        mk, mv = _memkv(mem_prompt.reshape(batch * N_MEM, D_MODEL), row(ln_mem),
                        w_mem_kv[l].astype(BF16))
        yp = _ffn(yp, row(ln_ffn1_pre), row(ln_ffn1_post), w1i, w1o, FFN_TM)
        yp, kl, vl, cl, hl = _mix_prompt(yp, mk, mv, swa_sinks[l], row(ln_mix_pre), w_tok, w_feat, wo,
                                         row(ln_mix_post), *rnn, batch, seq, MIX_TQ)
        yp = _ffn(yp, row(ln_ffn2_pre), row(ln_ffn2_post), w2i, w2o, FFN_TM)
        outs[0].append(kl.reshape(batch, WINDOW, SWA_KV_HEADS, HEAD_DIM))
        outs[1].append(vl.reshape(batch, WINDOW, SWA_KV_HEADS, HEAD_DIM))
        outs[2].append(mk.reshape(batch, N_MEM, MEM_HEADS, HEAD_DIM))
        outs[3].append(mv.reshape(batch, N_MEM, MEM_HEADS, HEAD_DIM))
        outs[4].append(cl)
        outs[5].append(hl.reshape(batch, RNN_WIDTH))

        ys = _ffn(ys, row(ln_ffn1_pre), row(ln_ffn1_post), w1i, w1o, n_dec)
        qswa, knew, vnew, qmem, ornn, newconv, hnew = _sample_pre(
            ys, row(ln_mix_pre), wi, state_conv[l].reshape(n_dec, -1), state_rglru_h[l], *rnn)
        sink = jnp.broadcast_to(swa_sinks[l][:, None], (SWA_HEADS, LANES))
        oswa, omem, nk, nv = _sample_attn(
            sink, qswa.transpose(1, 0, 2), knew, vnew, qmem.transpose(1, 0, 2),
            cache_swa_k[l].reshape(n_dec, WINDOW, KV_SWA), cache_swa_v[l].reshape(n_dec, WINDOW, KV_SWA),
            cache_mem_k[l].reshape(n_dec, N_MEM, Q_MEM), cache_mem_v[l].reshape(n_dec, N_MEM, Q_MEM),
            SAMPLE_SB)
        ys = _sample_post(ys, oswa.transpose(1, 0, 2), omem.transpose(1, 0, 2), ornn, wo,
                          row(ln_mix_post))
        ys = _ffn(ys, row(ln_ffn2_pre), row(ln_ffn2_post), w2i, w2o, n_dec)
        outs[6].append(nk.reshape(n_dec, WINDOW, SWA_KV_HEADS, HEAD_DIM))
        outs[7].append(nv.reshape(n_dec, WINDOW, SWA_KV_HEADS, HEAD_DIM))
        outs[8].append(newconv.reshape(n_dec, CONV_WIDTH - 1, RNN_WIDTH))
        outs[9].append(hnew)
    return (yp.reshape(batch, seq, D_MODEL), ys.reshape(n_dec, 1, D_MODEL),
            *[jnp.stack(o) for o in outs])
```

```python
import jax
import jax.numpy as jnp
from jax import lax
from jax.experimental import pallas as pl
from jax.experimental.pallas import tpu as pltpu

F32 = jnp.float32
BF16 = jnp.bfloat16

D_MODEL = 1024
HEAD_DIM = 64
SWA_HEADS = 8
SWA_KV_HEADS = 2
SWA_GROUP = SWA_HEADS // SWA_KV_HEADS
WINDOW = 128
N_MEM = 256
MEM_HEADS = 4
RNN_WIDTH = 256
RNN_BLOCKS = 4
RNN_BLOCK_DIM = RNN_WIDTH // RNN_BLOCKS
CONV_WIDTH = 4
RG_C = 8.0
D_FF = 2816
EPS = 1e-6
NEG = -1e30

Q_SWA = SWA_HEADS * HEAD_DIM
KV_SWA = SWA_KV_HEADS * HEAD_DIM
Q_MEM = MEM_HEADS * HEAD_DIM
D_IN = Q_SWA + 2 * KV_SWA + Q_MEM + 2 * RNN_WIDTH
D_MIX = Q_SWA + Q_MEM + RNN_WIDTH
C_K = Q_SWA
C_V = C_K + KV_SWA
C_QM = C_V + KV_SWA
C_XR = C_QM + Q_MEM
C_GR = C_XR + RNN_WIDTH
SCALE = HEAD_DIM ** -0.5
LOG2E = 1.4426950408889634

D_TOK = KV_SWA + 2 * RNN_WIDTH
D_FEAT = Q_SWA + KV_SWA + Q_MEM
R_V = Q_SWA
R_QM = Q_SWA + KV_SWA
D_ATT = Q_SWA + Q_MEM

LANES = 128
SUBLANES = 8
VMEM_LIMIT = 56 * 1024 * 1024

FFN_TM = 1024
FFN_SUB = 256
MIX_CHAINS = 1
MIX_TQ = 512
SAMPLE_SB = 8


def _dot(a, b):
    return jnp.dot(a, b, preferred_element_type=F32)


def _dot_nt(a, b):
    return lax.dot_general(a, b, (((1,), (1,)), ((), ())), preferred_element_type=F32)


def _dot_tn(a, b):
    return lax.dot_general(a, b, (((0,), (0,)), ((), ())), preferred_element_type=F32)


def _rms(x, g):
    return x * lax.rsqrt(jnp.mean(x * x, axis=-1, keepdims=True) + EPS) * g


def _softplus(x):
    return jnp.maximum(x, 0.0) + jnp.log1p(jnp.exp(-jnp.abs(x)))


def _const_spec(shape):
    return pl.BlockSpec(shape, lambda *_: (0,) * len(shape), pipeline_mode=pl.Buffered(1))


def _params(n_grid):
    return pltpu.CompilerParams(dimension_semantics=("arbitrary",) * n_grid,
                                vmem_limit_bytes=VMEM_LIMIT)


def _ffn_kernel(x_ref, gpre_ref, gpost_ref, win_ref, wout_ref, o_ref):
    tm = x_ref.shape[0]
    sub = min(tm, FFN_SUB)
    for r0 in range(0, tm, sub):
        x = x_ref[r0:r0 + sub, :]
        xn = _rms(x, gpre_ref[...]).astype(BF16)
        gate = _dot(xn, win_ref[:, :D_FF])
        up = _dot(xn, win_ref[:, D_FF:])
        act = (gate * jax.nn.sigmoid(gate) * up).astype(BF16)
        y = _dot(act, wout_ref[...])
        o_ref[r0:r0 + sub, :] = x + 0.5 * _rms(y, gpost_ref[...])


def _ffn(x, g_pre, g_post, w_in, w_out, tm):
    m = x.shape[0]
    row = pl.BlockSpec((tm, D_MODEL), lambda i: (i, 0))
    return pl.pallas_call(
        _ffn_kernel,
        grid=(m // tm,),
        in_specs=[row, _const_spec((1, D_MODEL)), _const_spec((1, D_MODEL)),
                  _const_spec((D_MODEL, 2 * D_FF)), _const_spec((D_FF, D_MODEL))],
        out_specs=row,
        out_shape=jax.ShapeDtypeStruct((m, D_MODEL), F32),
        compiler_params=_params(1),
        name="ffn",
    )(x, g_pre, g_post, w_in, w_out)


def _memkv_kernel(m_ref, g_ref, w_ref, k_ref, v_ref):
    xn = _rms(m_ref[...], g_ref[...]).astype(BF16)
    kv = _dot(xn, w_ref[...])
    k_ref[...] = kv[:, :Q_MEM]
    v_ref[...] = kv[:, Q_MEM:]


def _memkv(mem, g, w):
    m = mem.shape[0]
    tm = 512
    out = pl.BlockSpec((tm, Q_MEM), lambda i: (i, 0))
    return pl.pallas_call(
        _memkv_kernel,
        grid=(m // tm,),
        in_specs=[pl.BlockSpec((tm, D_MODEL), lambda i: (i, 0)),
                  _const_spec((1, D_MODEL)), _const_spec((D_MODEL, 2 * Q_MEM))],
        out_specs=[out, out],
        out_shape=[jax.ShapeDtypeStruct((m, Q_MEM), F32)] * 2,
        compiler_params=_params(1),
        name="memkv",
    )(mem, g, w)


def _rglru_gates(u, wg_ref, bg_ref, lam_ref):
    gates = jax.nn.sigmoid(_dot(u.astype(BF16), wg_ref[...]) + bg_ref[...])
    r = gates[:, :RNN_WIDTH]
    i = gates[:, RNN_WIDTH:]
    log_a = -RG_C * r * _softplus(-lam_ref[...])
    a = jnp.exp(log_a)
    b = jnp.sqrt(1.0 - a * a) * (i * u)
    return a, b


def _head_mask(width, head):
    lane = lax.broadcasted_iota(jnp.int32, (1, width), 1)
    return (lane >= head * HEAD_DIM) & (lane < (head + 1) * HEAD_DIM)


def _mix_prompt_kernel(sinks_ref, x_ref, mk_ref, mv_ref, *refs):
    n_shared = 10
    shared, outs, scratch = refs[:n_shared], refs[n_shared:n_shared + 5], refs[n_shared + 5:]
    bias_scr = scratch[4]
    gw = SWA_GROUP * WINDOW

    @pl.when((pl.program_id(0) == 0) & (pl.program_id(1) == 0))
    def _():
        ki = lax.broadcasted_iota(jnp.int32, (2 * WINDOW, gw), 0)
        qi = lax.broadcasted_iota(jnp.int32, (2 * WINDOW, gw), 1) & (WINDOW - 1)
        band = (ki >= qi) & (ki <= qi + WINDOW)
        bias_scr[0] = jnp.where(band, 0.0, NEG)
        bias_scr[1] = jnp.where(band & (ki >= WINDOW), 0.0, NEG)

    chains = []
    for c in range(x_ref.shape[1]):
        chain_scratch = [r if r is bias_scr else r.at[c] for r in scratch]
        chains.append(_mix_chain(sinks_ref, x_ref.at[0, c], mk_ref.at[0, c], mv_ref.at[0, c], *shared,
                                 *[r.at[0, c] for r in outs], *chain_scratch))
    for phase in range(3):
        for chain in chains:
            next(chain, None)


def _mix_chain(sinks_ref, x_ref, mk_ref, mv_ref, gpre_ref, wtok_ref, wfeat_ref, wout_ref,
               gpost_ref, convw_ref, convb_ref, wg_ref, bg_ref, lam_ref,
               y_ref, klast_ref, vlast_ref, convlast_ref, hlast_ref,
               k_scr, vt_scr, mkb_scr, mvt_scr, bias_scr, xr_scr, xtail_scr, h_scr,
               hs_scr, ps_scr, ot_scr):
    t = pl.program_id(1)
    tq = x_ref.shape[0]
    nblk = tq // WINDOW
    chunk = tq // SUBLANES
    gw = SWA_GROUP * WINDOW

    @pl.when(t == 0)
    def _():
        k_scr[0:WINDOW, :] = jnp.zeros((WINDOW, KV_SWA), BF16)
        vt_scr[:, 0:WINDOW] = jnp.zeros((KV_SWA, WINDOW), BF16)
        xtail_scr[...] = jnp.zeros(xtail_scr.shape, F32)
        h_scr[...] = jnp.zeros((1, RNN_WIDTH), F32)
        mkb_scr[...] = mk_ref[...].astype(BF16)
        mvt_scr[...] = mv_ref[...].T.astype(BF16)

    yield
    x = x_ref[...]
    xn = _rms(x, gpre_ref[...]).astype(BF16)
    ptok = _dot(xn, wtok_ref[...])
    k = ptok[:, :KV_SWA]
    k_scr[WINDOW:, :] = k.astype(BF16)

    def feat(r0, r1):
        return _dot_nt(wfeat_ref[r0:r1, :], xn)

    halves = RNN_WIDTH // LANES
    pitch = chunk + SUBLANES
    taps = CONV_WIDTH - 1

    def put(ref, val):
        for hf in range(halves):
            for c in range(SUBLANES):
                ref[hf, c * pitch:c * pitch + chunk, :] = (
                    val[c * chunk:(c + 1) * chunk, hf * LANES:(hf + 1) * LANES])

    def get(ref):
        return jnp.concatenate(
            [jnp.concatenate([ref[hf, c * pitch:c * pitch + chunk, :] for c in range(SUBLANES)], axis=0)
             for hf in range(halves)], axis=1)

    def get_perm(ref, r):
        return jnp.concatenate([ref[hf, pl.ds(r, SUBLANES, stride=pitch), :] for hf in range(halves)],
                               axis=1)

    def put_perm(ref, r, val):
        for hf in range(halves):
            ref[hf, pl.ds(r, SUBLANES, stride=pitch), :] = val[:, hf * LANES:(hf + 1) * LANES]

    put(xr_scr, ptok[:, KV_SWA:KV_SWA + RNN_WIDTH])
    xs = [get_perm(xr_scr, r) for r in range(chunk)]
    sub = lax.broadcasted_iota(jnp.int32, (SUBLANES, RNN_WIDTH), 0)
    head_rows = [pltpu.roll(jnp.where(sub == SUBLANES - 1, xtail_scr[i], xs[chunk - taps + i]), 1, 0)
                 for i in range(taps)]
    for i in range(taps):
        xtail_scr[i] = xs[chunk - taps + i]
    xm = head_rows + xs
    w = [convw_ref[jj:jj + 1, :] for jj in range(CONV_WIDTH)]
    us = []
    for r in range(chunk):
        u = convb_ref[...] + xm[r] * w[0]
        for jj in range(1, CONV_WIDTH):
            u = u + xm[r + jj] * w[jj]
        us.append(u)
    qt = [(feat(0, Q_SWA // 2) * (SCALE * LOG2E)).astype(BF16)]
    a, b = _rglru_gates(jnp.concatenate(us, axis=0), wg_ref, bg_ref, lam_ref)
    qt.append((feat(Q_SWA // 2, Q_SWA) * (SCALE * LOG2E)).astype(BF16))
    ps = hs = None
    for r in range(chunk):
        ar = a[r * SUBLANES:(r + 1) * SUBLANES, :]
        br = b[r * SUBLANES:(r + 1) * SUBLANES, :]
        if r == 0:
            ps, hs = ar, br
        else:
            hs = ar * hs + br
            ps = ar * ps
        put_perm(hs_scr, r, hs)
        put_perm(ps_scr, r, ps)
    h_in = h_scr[...]
    h_in_rows = []
    for c in range(SUBLANES):
        h_in_rows.append(jnp.broadcast_to(h_in, (chunk, RNN_WIDTH)))
        h_in = ps[c:c + 1, :] * h_in + hs[c:c + 1, :]
    h_scr[...] = h_in
    pvq = feat(R_V, D_FEAT)
    vt = pvq[:KV_SWA, :]
    vt_scr[:, WINDOW:] = vt.astype(BF16)
    qm = (pvq[KV_SWA:, :] * (SCALE * LOG2E)).astype(BF16)
    h = get(hs_scr) + get(ps_scr) * jnp.concatenate(h_in_rows, axis=0)
    o_rnn = (h * jax.nn.gelu(ptok[:, KV_SWA + RNN_WIDTH:])).astype(BF16)

    cols = []
    for head in range(MEM_HEADS):
        parts = [qm[head * HEAD_DIM:(head + 1) * HEAD_DIM, :]]
        if head > 0:
            parts.insert(0, jnp.zeros((head * HEAD_DIM, tq), BF16))
        if head < MEM_HEADS - 1:
            parts.append(jnp.zeros(((MEM_HEADS - 1 - head) * HEAD_DIM, tq), BF16))
        cols.append(jnp.concatenate(parts, axis=0))
    s_mem = _dot(mkb_scr[...], jnp.concatenate(cols, axis=1))

    def mem_softmax():
        m = jnp.max(s_mem, axis=0, keepdims=True)
        e = jnp.exp2(s_mem - m)
        return e.astype(BF16), 1.0 / jnp.sum(e, axis=0, keepdims=True)

    def mem_pv(eb, inv_l):
        for head in range(MEM_HEADS):
            o = _dot(mvt_scr[head * HEAD_DIM:(head + 1) * HEAD_DIM, :], eb[:, head * tq:(head + 1) * tq])
            ot_scr[Q_SWA + head * HEAD_DIM:Q_SWA + (head + 1) * HEAD_DIM, :] = (
                o * inv_l[:, head * tq:(head + 1) * tq]).astype(BF16)

    zero_head = jnp.zeros((HEAD_DIM, WINDOW), BF16)
    seg = lax.broadcasted_iota(jnp.int32, (1, gw), 1) // WINDOW
    for j in range(nblk):
        c0 = j * WINDOW
        bias = bias_scr[jnp.where(t == 0, 1, 0)] if j == 0 else bias_scr[0]
        for kv in range(SWA_KV_HEADS):
            cols = []
            sink = jnp.zeros((1, gw), F32)
            for g in range(SWA_GROUP):
                head = kv * SWA_GROUP + g
                qh = qt[kv][g * HEAD_DIM:(g + 1) * HEAD_DIM, c0:c0 + WINDOW]
                cols.append(jnp.concatenate([qh, zero_head] if kv == 0 else [zero_head, qh], axis=0))
                sink = jnp.where(seg == g, sinks_ref[head] * LOG2E, sink)
            rhs = jnp.concatenate(cols, axis=1)
            s = _dot(k_scr[c0:c0 + 2 * WINDOW, :], rhs) + bias
            m = jnp.maximum(jnp.max(s, axis=0, keepdims=True), sink)
            e = jnp.exp2(s - m)
            l = jnp.sum(e, axis=0, keepdims=True) + jnp.exp2(sink - m)
            o = _dot(vt_scr[kv * HEAD_DIM:(kv + 1) * HEAD_DIM, c0:c0 + 2 * WINDOW], e.astype(BF16))
            o = o * (1.0 / l)
            for g in range(SWA_GROUP):
                head = kv * SWA_GROUP + g
                ot_scr[head * HEAD_DIM:(head + 1) * HEAD_DIM, c0:c0 + WINDOW] = (
                    o[:, g * WINDOW:(g + 1) * WINDOW].astype(BF16))
        if j == 0:
            mem_p = mem_softmax()
        if j == min(1, nblk - 1):
            mem_pv(*mem_p)

    o = _dot_tn(ot_scr[...], wout_ref[:D_ATT, :]) + _dot(o_rnn, wout_ref[D_ATT:, :])
    y_ref[...] = x + _rms(o, gpost_ref[...])

    k_scr[0:WINDOW, :] = k_scr[tq:tq + WINDOW, :]
    vt_scr[:, 0:WINDOW] = vt_scr[:, tq:tq + WINDOW]

    yield
    @pl.when(t == pl.num_programs(1) - 1)
    def _():
        klast_ref[...] = k[tq - WINDOW:, :]
        vlast_ref[...] = vt[:, tq - WINDOW:].T
        last = (SUBLANES - 1) * pitch + chunk
        convlast_ref[...] = jnp.concatenate(
            [xr_scr[hf, last - taps:last, :] for hf in range(halves)], axis=1)
        hlast_ref[...] = h_in


def _mix_prompt(x, mk, mv, sinks, g_pre, w_tok, w_feat, w_out, g_post, conv_w, conv_b, wg, bg, lam,
                batch, seq, tq):
    nt = seq // tq
    nc = MIX_CHAINS
    groups = batch // nc
    row = pl.BlockSpec((1, nc, tq, D_MODEL), lambda b, t: (b, 0, t, 0))
    per_b = lambda shape: pl.BlockSpec((1, nc) + shape, lambda b, t: (b, 0) + (0,) * len(shape))
    sds = lambda *shape: jax.ShapeDtypeStruct((groups, nc) + shape, F32)
    vmem = lambda shape, dtype: pltpu.VMEM((nc,) + shape, dtype)
    outs = pl.pallas_call(
        _mix_prompt_kernel,
        grid=(groups, nt),
        in_specs=[pl.BlockSpec(memory_space=pltpu.SMEM),
                  row, per_b((N_MEM, Q_MEM)), per_b((N_MEM, Q_MEM)),
                  _const_spec((1, D_MODEL)), _const_spec((D_MODEL, D_TOK)),
                  _const_spec((D_FEAT, D_MODEL)),
                  _const_spec((D_MIX, D_MODEL)), _const_spec((1, D_MODEL)),
                  _const_spec((CONV_WIDTH, RNN_WIDTH)), _const_spec((1, RNN_WIDTH)),
                  _const_spec((RNN_WIDTH, 2 * RNN_WIDTH)), _const_spec((1, 2 * RNN_WIDTH)),
                  _const_spec((1, RNN_WIDTH))],
        out_specs=[row, per_b((WINDOW, KV_SWA)), per_b((WINDOW, KV_SWA)),
                   per_b((CONV_WIDTH - 1, RNN_WIDTH)), per_b((1, RNN_WIDTH))],
        out_shape=[sds(seq, D_MODEL), sds(WINDOW, KV_SWA), sds(WINDOW, KV_SWA),
                   sds(CONV_WIDTH - 1, RNN_WIDTH), sds(1, RNN_WIDTH)],
        scratch_shapes=[vmem((WINDOW + tq, KV_SWA), BF16),
                        vmem((KV_SWA, WINDOW + tq), BF16),
                        vmem((N_MEM, Q_MEM), BF16),
                        vmem((Q_MEM, N_MEM), BF16),
                        pltpu.VMEM((2, 2 * WINDOW, SWA_GROUP * WINDOW), F32),
                        vmem((RNN_WIDTH // LANES, tq + SUBLANES * SUBLANES, LANES), F32),
                        vmem((CONV_WIDTH - 1, SUBLANES, RNN_WIDTH), F32),
                        vmem((1, RNN_WIDTH), F32),
                        vmem((RNN_WIDTH // LANES, tq + SUBLANES * SUBLANES, LANES), F32),
                        vmem((RNN_WIDTH // LANES, tq + SUBLANES * SUBLANES, LANES), F32),
                        vmem((D_ATT, tq), BF16)],
        compiler_params=_params(2),
        name="mix_prompt",
    )(sinks, x.reshape(groups, nc, seq, D_MODEL), mk.reshape(groups, nc, N_MEM, Q_MEM),
      mv.reshape(groups, nc, N_MEM, Q_MEM), g_pre, w_tok, w_feat, w_out, g_post, conv_w, conv_b, wg, bg, lam)
    y, kl, vl, cl, hl = outs
    return (y.reshape(batch * seq, D_MODEL), kl.reshape(batch * WINDOW, KV_SWA),
            vl.reshape(batch * WINDOW, KV_SWA), cl.reshape(batch, CONV_WIDTH - 1, RNN_WIDTH),
            hl.reshape(batch, 1, RNN_WIDTH))


def _sample_pre_kernel(x_ref, gpre_ref, win_ref, conv_ref, h0_ref, convw_ref, convb_ref,
                       wg_ref, bg_ref, lam_ref,
                       qswa_ref, knew_ref, vnew_ref, qmem_ref, ornn_ref, newconv_ref, hnew_ref):
    xn = _rms(x_ref[...], gpre_ref[...]).astype(BF16)
    proj = _dot(xn, win_ref[...])
    lane = lax.broadcasted_iota(jnp.int32, (1, LANES), 1)
    lo = lane < HEAD_DIM
    for head in range(SWA_HEADS):
        kv = head // SWA_GROUP
        c0 = (head // 2) * LANES
        qt = proj[:, c0:c0 + LANES] * SCALE
        if head % 2 != kv:
            qt = pltpu.roll(qt, HEAD_DIM, 1)
        keep = lo if kv == 0 else jnp.logical_not(lo)
        qswa_ref[head] = jnp.where(keep, qt, 0.0)
    knew_ref[...] = proj[:, C_K:C_V]
    vnew_ref[...] = proj[:, C_V:C_QM]
    qm = proj[:, C_QM:C_XR] * SCALE
    for head in range(MEM_HEADS):
        qmem_ref[head] = jnp.where(_head_mask(Q_MEM, head), qm, 0.0)
    for head in range(MEM_HEADS, 8):
        qmem_ref[head] = jnp.zeros_like(qm)

    xr = proj[:, C_XR:C_GR]
    u = convb_ref[...] + xr * convw_ref[CONV_WIDTH - 1:CONV_WIDTH, :]
    for jj in range(CONV_WIDTH - 1):
        u = u + conv_ref[:, jj * RNN_WIDTH:(jj + 1) * RNN_WIDTH] * convw_ref[jj:jj + 1, :]
    newconv_ref[:, :(CONV_WIDTH - 2) * RNN_WIDTH] = conv_ref[:, RNN_WIDTH:]
    newconv_ref[:, (CONV_WIDTH - 2) * RNN_WIDTH:] = xr
    a, b = _rglru_gates(u, wg_ref, bg_ref, lam_ref)
    h = a * h0_ref[...] + b
    hnew_ref[...] = h
    ornn_ref[...] = h * jax.nn.gelu(proj[:, C_GR:])


def _sample_pre(x, g_pre, w_in, conv, h0, conv_w, conv_b, wg, bg, lam):
    n = x.shape[0]
    sds = lambda *shape: jax.ShapeDtypeStruct(shape, F32)
    return pl.pallas_call(
        _sample_pre_kernel,
        out_shape=[sds(SWA_HEADS, n, LANES), sds(n, KV_SWA), sds(n, KV_SWA), sds(8, n, Q_MEM),
                   sds(n, RNN_WIDTH), sds(n, (CONV_WIDTH - 1) * RNN_WIDTH), sds(n, RNN_WIDTH)],
        compiler_params=pltpu.CompilerParams(vmem_limit_bytes=VMEM_LIMIT),
        name="sample_pre",
    )(x, g_pre, w_in, conv, h0, conv_w, conv_b, wg, bg, lam)


def _sample_attn_kernel(sink_ref, q_ref, knew_ref, vnew_ref, qm_ref, ck_ref, cv_ref, cmk_ref, cmv_ref,
                        oswa_ref, omem_ref, nk_ref, nv_ref):
    sb, nh = q_ref.shape[0], q_ref.shape[1]
    rows = lambda b: slice(b * nh, (b + 1) * nh)
    stack = lambda f: jnp.concatenate([f(b) for b in range(sb)], axis=0)
    per_head = lambda ref, b: jnp.broadcast_to(ref[b:b + 1, :], (nh, ref.shape[1]))
    q = stack(lambda b: q_ref[b])
    sink = stack(lambda b: sink_ref[:, 0:1])
    s = stack(lambda b: _dot_nt(q_ref[b].astype(BF16), ck_ref[b].astype(BF16)))
    s_new = jnp.sum(q * stack(lambda b: per_head(knew_ref, b)), axis=-1, keepdims=True)
    m = jnp.maximum(jnp.maximum(jnp.max(s, axis=-1, keepdims=True), s_new), sink)
    e = jnp.exp(s - m)
    e_new = jnp.exp(s_new - m)
    l = jnp.sum(e, axis=-1, keepdims=True) + e_new + jnp.exp(sink - m)
    o = stack(lambda b: _dot(e[rows(b)].astype(BF16), cv_ref[b].astype(BF16)))
    o = (o + e_new * stack(lambda b: per_head(vnew_ref, b))) / l
    for b in range(sb):
        oswa_ref[b] = o[rows(b)]
        nk_ref[b, 0:WINDOW - 1, :] = ck_ref[b, 1:WINDOW, :]
        nk_ref[b, WINDOW - 1:WINDOW, :] = knew_ref[b:b + 1, :]
        nv_ref[b, 0:WINDOW - 1, :] = cv_ref[b, 1:WINDOW, :]
        nv_ref[b, WINDOW - 1:WINDOW, :] = vnew_ref[b:b + 1, :]

    sm = stack(lambda b: _dot_nt(qm_ref[b].astype(BF16), cmk_ref[b].astype(BF16)))
    em = jnp.exp(sm - jnp.max(sm, axis=-1, keepdims=True))
    lm = jnp.sum(em, axis=-1, keepdims=True)
    om = stack(lambda b: _dot(em[rows(b)].astype(BF16), cmv_ref[b].astype(BF16))) / lm
    for b in range(sb):
        omem_ref[b] = om[rows(b)]


def _sample_attn(sink, q, knew, vnew, qm, ck, cv, cmk, cmv, sb):
    n = q.shape[0]
    blk = lambda *shape: pl.BlockSpec((sb,) + shape, lambda i: (i,) + (0,) * len(shape))
    sds = lambda *shape: jax.ShapeDtypeStruct(shape, F32)
    return pl.pallas_call(
        _sample_attn_kernel,
        grid=(n // sb,),
        in_specs=[_const_spec((SWA_HEADS, LANES)),
                  blk(SWA_HEADS, LANES), blk(KV_SWA), blk(KV_SWA), blk(8, Q_MEM),
                  blk(WINDOW, KV_SWA), blk(WINDOW, KV_SWA), blk(N_MEM, Q_MEM), blk(N_MEM, Q_MEM)],
        out_specs=[blk(SWA_HEADS, LANES), blk(8, Q_MEM), blk(WINDOW, KV_SWA), blk(WINDOW, KV_SWA)],
        out_shape=[sds(n, SWA_HEADS, LANES), sds(n, 8, Q_MEM),
                   sds(n, WINDOW, KV_SWA), sds(n, WINDOW, KV_SWA)],
        compiler_params=_params(1),
        name="sample_attn",
    )(sink, q, knew, vnew, qm, ck, cv, cmk, cmv)


def _sample_post_kernel(x_ref, oswa_ref, omem_ref, ornn_ref, wout_ref, gpost_ref, y_ref):
    lane = lax.broadcasted_iota(jnp.int32, (1, LANES), 1)
    lo = lane < HEAD_DIM
    tiles = []
    for pair in range(SWA_HEADS // 2):
        kv = (2 * pair) // SWA_GROUP
        low = oswa_ref[2 * pair]
        high = oswa_ref[2 * pair + 1]
        if kv == 1:
            low = pltpu.roll(low, HEAD_DIM, 1)
        else:
            high = pltpu.roll(high, HEAD_DIM, 1)
        tiles.append(jnp.where(lo, low, high))
    o_mem = jnp.zeros(omem_ref.shape[1:], F32)
    for head in range(MEM_HEADS):
        o_mem = jnp.where(_head_mask(Q_MEM, head), omem_ref[head], o_mem)
    ocat = jnp.concatenate(tiles + [o_mem, ornn_ref[...]], axis=-1).astype(BF16)
    o = _dot(ocat, wout_ref[...])
    y_ref[...] = x_ref[...] + _rms(o, gpost_ref[...])


def _sample_post(x, oswa, omem, ornn, w_out, g_post):
    return pl.pallas_call(
        _sample_post_kernel,
        out_shape=jax.ShapeDtypeStruct(x.shape, F32),
        compiler_params=pltpu.CompilerParams(vmem_limit_bytes=VMEM_LIMIT),
        name="sample_post",
    )(x, oswa, omem, ornn, w_out, g_post)


def _block_diag(w):
    nb, d, _ = w.shape
    eye = jnp.eye(nb, dtype=w.dtype)
    return (eye[:, None, :, None] * w[:, :, None, :]).reshape(nb * d, nb * d)


def kernel(x_prompt, x_sample, mem_prompt, cache_swa_k, cache_swa_v, cache_mem_k, cache_mem_v, state_conv, state_rglru_h, ln_ffn1_pre, ln_ffn1_post, w_ffn1_in, w_ffn1_out, ln_mix_pre, ln_mix_post, w_in, w_out, swa_sinks, conv_w, conv_b, rg_wa, rg_ba, rg_wx, rg_bx, rg_lambda, ln_mem, w_mem_kv, ln_ffn2_pre, ln_ffn2_post, w_ffn2_in, w_ffn2_out):
    batch, seq, _ = x_prompt.shape
    n_dec = x_sample.shape[0]
    depth = w_in.shape[0]
    yp = x_prompt.reshape(batch * seq, D_MODEL)
    ys = x_sample.reshape(n_dec, D_MODEL)
    outs = [[] for _ in range(10)]
    for l in range(depth):
        row = lambda a: a[l].reshape(1, -1)
        w1i, w1o = w_ffn1_in[l].astype(BF16), w_ffn1_out[l].astype(BF16)
        w2i, w2o = w_ffn2_in[l].astype(BF16), w_ffn2_out[l].astype(BF16)
        wi, wo = w_in[l].astype(BF16), w_out[l].astype(BF16)
        w_tok = jnp.concatenate([wi[:, C_K:C_V], wi[:, C_XR:]], axis=1)
        w_feat = jnp.concatenate([wi[:, :C_K], wi[:, C_V:C_XR]], axis=1).T
        wg = jnp.concatenate([_block_diag(rg_wa[l]), _block_diag(rg_wx[l])], axis=1).astype(BF16)
        bg = jnp.concatenate([row(rg_ba), row(rg_bx)], axis=1)
        rnn = (conv_w[l], row(conv_b), wg, bg, row(rg_lambda))

        mk, mv = _memkv(mem_prompt.reshape(batch * N_MEM, D_MODEL), row(ln_mem),
                        w_mem_kv[l].astype(BF16))
        yp = _ffn(yp, row(ln_ffn1_pre), row(ln_ffn1_post), w1i, w1o, FFN_TM)
        yp, kl, vl, cl, hl = _mix_prompt(yp, mk, mv, swa_sinks[l], row(ln_mix_pre), w_tok, w_feat, wo,
                                         row(ln_mix_post), *rnn, batch, seq, MIX_TQ)
        yp = _ffn(yp, row(ln_ffn2_pre), row(ln_ffn2_post), w2i, w2o, FFN_TM)
        outs[0].append(kl.reshape(batch, WINDOW, SWA_KV_HEADS, HEAD_DIM))
        outs[1].append(vl.reshape(batch, WINDOW, SWA_KV_HEADS, HEAD_DIM))
        outs[2].append(mk.reshape(batch, N_MEM, MEM_HEADS, HEAD_DIM))
        outs[3].append(mv.reshape(batch, N_MEM, MEM_HEADS, HEAD_DIM))
        outs[4].append(cl)
        outs[5].append(hl.reshape(batch, RNN_WIDTH))

        ys = _ffn(ys, row(ln_ffn1_pre), row(ln_ffn1_post), w1i, w1o, n_dec)
        qswa, knew, vnew, qmem, ornn, newconv, hnew = _sample_pre(
            ys, row(ln_mix_pre), wi, state_conv[l].reshape(n_dec, -1), state_rglru_h[l], *rnn)
        sink = jnp.broadcast_to(swa_sinks[l][:, None], (SWA_HEADS, LANES))
        oswa, omem, nk, nv = _sample_attn(
            sink, qswa.transpose(1, 0, 2), knew, vnew, qmem.transpose(1, 0, 2),
            cache_swa_k[l].reshape(n_dec, WINDOW, KV_SWA), cache_swa_v[l].reshape(n_dec, WINDOW, KV_SWA),
            cache_mem_k[l].reshape(n_dec, N_MEM, Q_MEM), cache_mem_v[l].reshape(n_dec, N_MEM, Q_MEM),
            SAMPLE_SB)
        ys = _sample_post(ys, oswa.transpose(1, 0, 2), omem.transpose(1, 0, 2), ornn, wo,
                          row(ln_mix_post))
        ys = _ffn(ys, row(ln_ffn2_pre), row(ln_ffn2_post), w2i, w2o, n_dec)
        outs[6].append(nk.reshape(n_dec, WINDOW, SWA_KV_HEADS, HEAD_DIM))
        outs[7].append(nv.reshape(n_dec, WINDOW, SWA_KV_HEADS, HEAD_DIM))
        outs[8].append(newconv.reshape(n_dec, CONV_WIDTH - 1, RNN_WIDTH))
        outs[9].append(hnew)
    return (yp.reshape(batch, seq, D_MODEL), ys.reshape(n_dec, 1, D_MODEL),
            *[jnp.stack(o) for o in outs])
```

```python
import jax
import jax.numpy as jnp
from jax import lax
from jax.experimental import pallas as pl
from jax.experimental.pallas import tpu as pltpu

F32 = jnp.float32
BF16 = jnp.bfloat16

D_MODEL = 1024
HEAD_DIM = 64
SWA_HEADS = 8
SWA_KV_HEADS = 2
SWA_GROUP = SWA_HEADS // SWA_KV_HEADS
WINDOW = 128
N_MEM = 256
MEM_HEADS = 4
RNN_WIDTH = 256
RNN_BLOCKS = 4
RNN_BLOCK_DIM = RNN_WIDTH // RNN_BLOCKS
CONV_WIDTH = 4
RG_C = 8.0
D_FF = 2816
EPS = 1e-6
NEG = -1e30

Q_SWA = SWA_HEADS * HEAD_DIM
KV_SWA = SWA_KV_HEADS * HEAD_DIM
Q_MEM = MEM_HEADS * HEAD_DIM
D_IN = Q_SWA + 2 * KV_SWA + Q_MEM + 2 * RNN_WIDTH
D_MIX = Q_SWA + Q_MEM + RNN_WIDTH
C_K = Q_SWA
C_V = C_K + KV_SWA
C_QM = C_V + KV_SWA
C_XR = C_QM + Q_MEM
C_GR = C_XR + RNN_WIDTH
SCALE = HEAD_DIM ** -0.5
LOG2E = 1.4426950408889634

D_TOK = KV_SWA + 2 * RNN_WIDTH
D_FEAT = Q_SWA + KV_SWA + Q_MEM
R_V = Q_SWA
R_QM = Q_SWA + KV_SWA
D_ATT = Q_SWA + Q_MEM

LANES = 128
SUBLANES = 8
VMEM_LIMIT = 56 * 1024 * 1024

FFN_TM = 512
FFN_SUB = 256
MIX_CHAINS = 2
MIX_TQ = 512
SAMPLE_SB = 8


def _dot(a, b):
    return jnp.dot(a, b, preferred_element_type=F32)


def _dot_nt(a, b):
    return lax.dot_general(a, b, (((1,), (1,)), ((), ())), preferred_element_type=F32)


def _dot_tn(a, b):
    return lax.dot_general(a, b, (((0,), (0,)), ((), ())), preferred_element_type=F32)


def _rms(x, g):
    return x * lax.rsqrt(jnp.mean(x * x, axis=-1, keepdims=True) + EPS) * g


def _softplus(x):
    return jnp.maximum(x, 0.0) + jnp.log1p(jnp.exp(-jnp.abs(x)))


def _const_spec(shape):
    return pl.BlockSpec(shape, lambda *_: (0,) * len(shape), pipeline_mode=pl.Buffered(1))


def _params(n_grid):
    return pltpu.CompilerParams(dimension_semantics=("arbitrary",) * n_grid,
                                vmem_limit_bytes=VMEM_LIMIT)


def _ffn_kernel(x_ref, gpre_ref, gpost_ref, win_ref, wout_ref, o_ref):
    tm = x_ref.shape[0]
    sub = min(tm, FFN_SUB)
    for r0 in range(0, tm, sub):
        x = x_ref[r0:r0 + sub, :]
        xn = _rms(x, gpre_ref[...]).astype(BF16)
        gate = _dot(xn, win_ref[:, :D_FF])
        up = _dot(xn, win_ref[:, D_FF:])
        act = (gate * jax.nn.sigmoid(gate) * up).astype(BF16)
        y = _dot(act, wout_ref[...])
        o_ref[r0:r0 + sub, :] = x + 0.5 * _rms(y, gpost_ref[...])


def _ffn(x, g_pre, g_post, w_in, w_out, tm):
    m = x.shape[0]
    row = pl.BlockSpec((tm, D_MODEL), lambda i: (i, 0))
    return pl.pallas_call(
        _ffn_kernel,
        grid=(m // tm,),
        in_specs=[row, _const_spec((1, D_MODEL)), _const_spec((1, D_MODEL)),
                  _const_spec((D_MODEL, 2 * D_FF)), _const_spec((D_FF, D_MODEL))],
        out_specs=row,
        out_shape=jax.ShapeDtypeStruct((m, D_MODEL), F32),
        compiler_params=_params(1),
        name="ffn",
    )(x, g_pre, g_post, w_in, w_out)


def _memkv_kernel(m_ref, g_ref, wt_ref, kt_ref, vt_ref):
    xn = _rms(m_ref[0], g_ref[...]).astype(BF16)
    kvt = _dot_nt(wt_ref[...], xn)
    kt_ref[0] = kvt[:Q_MEM, :]
    vt_ref[0] = kvt[Q_MEM:, :]


def _memkv(mem, g, wt):
    batch = mem.shape[0]
    out = pl.BlockSpec((1, Q_MEM, N_MEM), lambda i: (i, 0, 0))
    return pl.pallas_call(
        _memkv_kernel,
        grid=(batch,),
        in_specs=[pl.BlockSpec((1, N_MEM, D_MODEL), lambda i: (i, 0, 0)),
                  _const_spec((1, D_MODEL)), _const_spec((2 * Q_MEM, D_MODEL))],
        out_specs=[out, out],
        out_shape=[jax.ShapeDtypeStruct((batch, Q_MEM, N_MEM), F32)] * 2,
        compiler_params=_params(1),
        name="memkv",
    )(mem, g, wt)


def _rglru_gates(u, wg_ref, bg_ref, lam_ref):
    gates = jax.nn.sigmoid(_dot(u.astype(BF16), wg_ref[...]) + bg_ref[...])
    r = gates[:, :RNN_WIDTH]
    i = gates[:, RNN_WIDTH:]
    log_a = -RG_C * r * _softplus(-lam_ref[...])
    a = jnp.exp(log_a)
    b = jnp.sqrt(1.0 - a * a) * (i * u)
    return a, b


def _head_mask(width, head):
    lane = lax.broadcasted_iota(jnp.int32, (1, width), 1)
    return (lane >= head * HEAD_DIM) & (lane < (head + 1) * HEAD_DIM)


def _mix_prompt_kernel(sinks_ref, x_ref, mk_ref, mv_ref, *refs):
    n_shared = 10
    shared, outs, scratch = refs[:n_shared], refs[n_shared:n_shared + 5], refs[n_shared + 5:]
    bias_scr = scratch[4]
    gw = SWA_GROUP * WINDOW

    @pl.when((pl.program_id(0) == 0) & (pl.program_id(1) == 0))
    def _():
        ki = lax.broadcasted_iota(jnp.int32, (2 * WINDOW, gw), 0)
        qi = lax.broadcasted_iota(jnp.int32, (2 * WINDOW, gw), 1) & (WINDOW - 1)
        band = (ki >= qi) & (ki <= qi + WINDOW)
        bias_scr[0] = jnp.where(band, 0.0, NEG)
        bias_scr[1] = jnp.where(band & (ki >= WINDOW), 0.0, NEG)

    chains = []
    for c in range(x_ref.shape[1]):
        chain_scratch = [r if r is bias_scr else r.at[c] for r in scratch]
        chains.append(_mix_chain(sinks_ref, x_ref.at[0, c], mk_ref.at[0, c], mv_ref.at[0, c], *shared,
                                 *[r.at[0, c] for r in outs], *chain_scratch))
    for phase in range(3):
        for chain in chains:
            next(chain, None)


def _mix_chain(sinks_ref, x_ref, mk_ref, mv_ref, gpre_ref, wtok_ref, wfeat_ref, wout_ref,
               gpost_ref, convw_ref, convb_ref, wg_ref, bg_ref, lam_ref,
               y_ref, klast_ref, vlast_ref, convlast_ref, hlast_ref,
               k_scr, vt_scr, mkb_scr, mvt_scr, bias_scr, xr_scr, xtail_scr, h_scr,
               hs_scr, ps_scr, ot_scr):
    t = pl.program_id(1)
    tq = x_ref.shape[0]
    nblk = tq // WINDOW
    chunk = tq // SUBLANES
    gw = SWA_GROUP * WINDOW

    @pl.when(t == 0)
    def _():
        k_scr[0:WINDOW, :] = jnp.zeros((WINDOW, KV_SWA), BF16)
        vt_scr[:, 0:WINDOW] = jnp.zeros((KV_SWA, WINDOW), BF16)
        xtail_scr[...] = jnp.zeros(xtail_scr.shape, F32)
        h_scr[...] = jnp.zeros((1, RNN_WIDTH), F32)
        mkb_scr[...] = mk_ref[...].T.astype(BF16)
        mvt_scr[...] = mv_ref[...].astype(BF16)

    yield
    x = x_ref[...]
    xn = _rms(x, gpre_ref[...]).astype(BF16)
    ptok = _dot(xn, wtok_ref[...])
    k = ptok[:, :KV_SWA]
    k_scr[WINDOW:, :] = k.astype(BF16)

    def feat(r0, r1):
        return _dot_nt(wfeat_ref[r0:r1, :], xn)

    halves = RNN_WIDTH // LANES
    pitch = chunk + SUBLANES
    taps = CONV_WIDTH - 1

    def put(ref, val):
        for hf in range(halves):
            for c in range(SUBLANES):
                ref[hf, c * pitch:c * pitch + chunk, :] = (
                    val[c * chunk:(c + 1) * chunk, hf * LANES:(hf + 1) * LANES])

    def get(ref):
        return jnp.concatenate(
            [jnp.concatenate([ref[hf, c * pitch:c * pitch + chunk, :] for c in range(SUBLANES)], axis=0)
             for hf in range(halves)], axis=1)

    def get_perm(ref, r):
        return jnp.concatenate([ref[hf, pl.ds(r, SUBLANES, stride=pitch), :] for hf in range(halves)],
                               axis=1)

    def put_perm(ref, r, val):
        for hf in range(halves):
            ref[hf, pl.ds(r, SUBLANES, stride=pitch), :] = val[:, hf * LANES:(hf + 1) * LANES]

    put(xr_scr, ptok[:, KV_SWA:KV_SWA + RNN_WIDTH])
    xs = [get_perm(xr_scr, r) for r in range(chunk)]
    sub = lax.broadcasted_iota(jnp.int32, (SUBLANES, RNN_WIDTH), 0)
    head_rows = [pltpu.roll(jnp.where(sub == SUBLANES - 1, xtail_scr[i], xs[chunk - taps + i]), 1, 0)
                 for i in range(taps)]
    for i in range(taps):
        xtail_scr[i] = xs[chunk - taps + i]
    xm = head_rows + xs
    w = [convw_ref[jj:jj + 1, :] for jj in range(CONV_WIDTH)]
    us = []
    for r in range(chunk):
        u = convb_ref[...] + xm[r] * w[0]
        for jj in range(1, CONV_WIDTH):
            u = u + xm[r + jj] * w[jj]
        us.append(u)
    qt = [(feat(0, Q_SWA // 2) * (SCALE * LOG2E)).astype(BF16)]
    a, b = _rglru_gates(jnp.concatenate(us, axis=0), wg_ref, bg_ref, lam_ref)
    qt.append((feat(Q_SWA // 2, Q_SWA) * (SCALE * LOG2E)).astype(BF16))
    ps = hs = None
    for r in range(chunk):
        ar = a[r * SUBLANES:(r + 1) * SUBLANES, :]
        br = b[r * SUBLANES:(r + 1) * SUBLANES, :]
        if r == 0:
            ps, hs = ar, br
        else:
            hs = ar * hs + br
            ps = ar * ps
        put_perm(hs_scr, r, hs)
        put_perm(ps_scr, r, ps)
    h_in = h_scr[...]
    h_in_rows = []
    for c in range(SUBLANES):
        h_in_rows.append(jnp.broadcast_to(h_in, (chunk, RNN_WIDTH)))
        h_in = ps[c:c + 1, :] * h_in + hs[c:c + 1, :]
    h_scr[...] = h_in
    pvq = feat(R_V, D_FEAT)
    vt = pvq[:KV_SWA, :]
    vt_scr[:, WINDOW:] = vt.astype(BF16)
    qm = (pvq[KV_SWA:, :] * (SCALE * LOG2E)).astype(BF16)
    h = get(hs_scr) + get(ps_scr) * jnp.concatenate(h_in_rows, axis=0)
    o_rnn = (h * jax.nn.gelu(ptok[:, KV_SWA + RNN_WIDTH:])).astype(BF16)
    o_acc = _dot(o_rnn, wout_ref[D_ATT:, :])

    cols = []
    for head in range(MEM_HEADS):
        parts = [qm[head * HEAD_DIM:(head + 1) * HEAD_DIM, :]]
        if head > 0:
            parts.insert(0, jnp.zeros((head * HEAD_DIM, tq), BF16))
        if head < MEM_HEADS - 1:
            parts.append(jnp.zeros(((MEM_HEADS - 1 - head) * HEAD_DIM, tq), BF16))
        cols.append(jnp.concatenate(parts, axis=0))
    s_mem = _dot(mkb_scr[...], jnp.concatenate(cols, axis=1))

    def mem_softmax():
        m = jnp.max(s_mem, axis=0, keepdims=True)
        e = jnp.exp2(s_mem - m)
        return e.astype(BF16), 1.0 / jnp.sum(e, axis=0, keepdims=True)

    def mem_pv(eb, inv_l):
        for head in range(MEM_HEADS):
            o = _dot(mvt_scr[head * HEAD_DIM:(head + 1) * HEAD_DIM, :], eb[:, head * tq:(head + 1) * tq])
            ot_scr[Q_SWA + head * HEAD_DIM:Q_SWA + (head + 1) * HEAD_DIM, :] = (
                o * inv_l[:, head * tq:(head + 1) * tq]).astype(BF16)

    zero_head = jnp.zeros((HEAD_DIM, WINDOW), BF16)
    seg = lax.broadcasted_iota(jnp.int32, (1, gw), 1) // WINDOW
    sinks = []
    for kv in range(SWA_KV_HEADS):
        sink = jnp.zeros((1, gw), F32)
        for g in range(SWA_GROUP):
            sink = jnp.where(seg == g, sinks_ref[kv * SWA_GROUP + g] * LOG2E, sink)
        sinks.append(sink)

    def swa_scores(j):
        c0 = j * WINDOW
        bias = bias_scr[jnp.where(t == 0, 1, 0)] if j == 0 else bias_scr[0]
        out = []
        for kv in range(SWA_KV_HEADS):
            cols = []
            for g in range(SWA_GROUP):
                qh = qt[kv][g * HEAD_DIM:(g + 1) * HEAD_DIM, c0:c0 + WINDOW]
                cols.append(jnp.concatenate([qh, zero_head] if kv == 0 else [zero_head, qh], axis=0))
            out.append(_dot(k_scr[c0:c0 + 2 * WINDOW, :], jnp.concatenate(cols, axis=1)) + bias)
        return out

    def swa_finish(j, scores):
        c0 = j * WINDOW
        kvs = range(SWA_KV_HEADS)
        m = [jnp.maximum(jnp.max(scores[kv], axis=0, keepdims=True), sinks[kv]) for kv in kvs]
        e = [jnp.exp2(scores[kv] - m[kv]) for kv in kvs]
        inv_l = [1.0 / (jnp.sum(e[kv], axis=0, keepdims=True) + jnp.exp2(sinks[kv] - m[kv])) for kv in kvs]
        for kv in kvs:
            o = _dot(vt_scr[kv * HEAD_DIM:(kv + 1) * HEAD_DIM, c0:c0 + 2 * WINDOW], e[kv].astype(BF16))
            o = o * inv_l[kv]
            for g in range(SWA_GROUP):
                head = kv * SWA_GROUP + g
                ot_scr[head * HEAD_DIM:(head + 1) * HEAD_DIM, c0:c0 + WINDOW] = (
                    o[:, g * WINDOW:(g + 1) * WINDOW].astype(BF16))

    scores = swa_scores(0)
    for j in range(nblk):
        nxt = swa_scores(j + 1) if j + 1 < nblk else None
        swa_finish(j, scores)
        scores = nxt
        if j == 0:
            mem_p = mem_softmax()
        if j == min(1, nblk - 1):
            mem_pv(*mem_p)

    o = _dot_tn(ot_scr[...], wout_ref[:D_ATT, :]) + o_acc
    y_ref[...] = x + _rms(o, gpost_ref[...])

    k_scr[0:WINDOW, :] = k_scr[tq:tq + WINDOW, :]
    vt_scr[:, 0:WINDOW] = vt_scr[:, tq:tq + WINDOW]

    yield
    @pl.when(t == pl.num_programs(1) - 1)
    def _():
        klast_ref[...] = k[tq - WINDOW:, :].T
        vlast_ref[...] = vt[:, tq - WINDOW:]
        last = (SUBLANES - 1) * pitch + chunk
        convlast_ref[...] = jnp.concatenate(
            [xr_scr[hf, last - taps:last, :] for hf in range(halves)], axis=1)
        hlast_ref[...] = h_in


def _mix_prompt(x, mk, mv, sinks, g_pre, w_tok, w_feat, w_out, g_post, conv_w, conv_b, wg, bg, lam,
                batch, seq, tq):
    nt = seq // tq
    nc = MIX_CHAINS
    groups = batch // nc
    row = pl.BlockSpec((1, nc, tq, D_MODEL), lambda b, t: (b, 0, t, 0))
    per_b = lambda shape: pl.BlockSpec((1, nc) + shape, lambda b, t: (b, 0) + (0,) * len(shape))
    sds = lambda *shape: jax.ShapeDtypeStruct((groups, nc) + shape, F32)
    vmem = lambda shape, dtype: pltpu.VMEM((nc,) + shape, dtype)
    outs = pl.pallas_call(
        _mix_prompt_kernel,
        grid=(groups, nt),
        in_specs=[pl.BlockSpec(memory_space=pltpu.SMEM),
                  row, per_b((N_MEM, Q_MEM)), per_b((N_MEM, Q_MEM)),
                  _const_spec((1, D_MODEL)), _const_spec((D_MODEL, D_TOK)),
                  _const_spec((D_FEAT, D_MODEL)),
                  _const_spec((D_MIX, D_MODEL)), _const_spec((1, D_MODEL)),
                  _const_spec((CONV_WIDTH, RNN_WIDTH)), _const_spec((1, RNN_WIDTH)),
                  _const_spec((RNN_WIDTH, 2 * RNN_WIDTH)), _const_spec((1, 2 * RNN_WIDTH)),
                  _const_spec((1, RNN_WIDTH))],
        out_specs=[row, per_b((WINDOW, KV_SWA)), per_b((WINDOW, KV_SWA)),
                   per_b((CONV_WIDTH - 1, RNN_WIDTH)), per_b((1, RNN_WIDTH))],
        out_shape=[sds(seq, D_MODEL), sds(WINDOW, KV_SWA), sds(WINDOW, KV_SWA),
                   sds(CONV_WIDTH - 1, RNN_WIDTH), sds(1, RNN_WIDTH)],
        scratch_shapes=[vmem((WINDOW + tq, KV_SWA), BF16),
                        vmem((KV_SWA, WINDOW + tq), BF16),
                        vmem((N_MEM, Q_MEM), BF16),
                        vmem((Q_MEM, N_MEM), BF16),
                        pltpu.VMEM((2, 2 * WINDOW, SWA_GROUP * WINDOW), F32),
                        vmem((RNN_WIDTH // LANES, tq + SUBLANES * SUBLANES, LANES), F32),
                        vmem((CONV_WIDTH - 1, SUBLANES, RNN_WIDTH), F32),
                        vmem((1, RNN_WIDTH), F32),
                        vmem((RNN_WIDTH // LANES, tq + SUBLANES * SUBLANES, LANES), F32),
                        vmem((RNN_WIDTH // LANES, tq + SUBLANES * SUBLANES, LANES), F32),
                        vmem((D_ATT, tq), BF16)],
        compiler_params=_params(2),
        name="mix_prompt",
    )(sinks, x.reshape(groups, nc, seq, D_MODEL), mk.reshape(groups, nc, N_MEM, Q_MEM),
      mv.reshape(groups, nc, N_MEM, Q_MEM), g_pre, w_tok, w_feat, w_out, g_post, conv_w, conv_b, wg, bg, lam)
    y, kl, vl, cl, hl = outs
    return (y.reshape(batch * seq, D_MODEL), kl.reshape(batch, KV_SWA, WINDOW),
            vl.reshape(batch, KV_SWA, WINDOW), cl.reshape(batch, CONV_WIDTH - 1, RNN_WIDTH),
            hl.reshape(batch, 1, RNN_WIDTH))


def _sample_pre_kernel(x_ref, gpre_ref, win_ref, conv_ref, h0_ref, convw_ref, convb_ref,
                       wg_ref, bg_ref, lam_ref,
                       qswa_ref, knew_ref, vnew_ref, qmem_ref, ornn_ref, newconv_ref, hnew_ref):
    xn = _rms(x_ref[...], gpre_ref[...]).astype(BF16)
    proj = _dot(xn, win_ref[...])
    lane = lax.broadcasted_iota(jnp.int32, (1, LANES), 1)
    lo = lane < HEAD_DIM
    for head in range(SWA_HEADS):
        kv = head // SWA_GROUP
        c0 = (head // 2) * LANES
        qt = proj[:, c0:c0 + LANES] * SCALE
        if head % 2 != kv:
            qt = pltpu.roll(qt, HEAD_DIM, 1)
        keep = lo if kv == 0 else jnp.logical_not(lo)
        qswa_ref[head] = jnp.where(keep, qt, 0.0)
    knew_ref[...] = proj[:, C_K:C_V]
    vnew_ref[...] = proj[:, C_V:C_QM]
    qm = proj[:, C_QM:C_XR] * SCALE
    for head in range(MEM_HEADS):
        qmem_ref[head] = jnp.where(_head_mask(Q_MEM, head), qm, 0.0)
    for head in range(MEM_HEADS, 8):
        qmem_ref[head] = jnp.zeros_like(qm)

    xr = proj[:, C_XR:C_GR]
    u = convb_ref[...] + xr * convw_ref[CONV_WIDTH - 1:CONV_WIDTH, :]
    for jj in range(CONV_WIDTH - 1):
        u = u + conv_ref[jj] * convw_ref[jj:jj + 1, :]
    for jj in range(CONV_WIDTH - 2):
        newconv_ref[jj] = conv_ref[jj + 1]
    newconv_ref[CONV_WIDTH - 2] = xr
    a, b = _rglru_gates(u, wg_ref, bg_ref, lam_ref)
    h = a * h0_ref[...] + b
    hnew_ref[...] = h
    ornn_ref[...] = h * jax.nn.gelu(proj[:, C_GR:])


def _sample_pre(x, g_pre, w_in, conv, h0, conv_w, conv_b, wg, bg, lam):
    n = x.shape[0]
    sds = lambda *shape: jax.ShapeDtypeStruct(shape, F32)
    return pl.pallas_call(
        _sample_pre_kernel,
        out_shape=[sds(SWA_HEADS, n, LANES), sds(n, KV_SWA), sds(n, KV_SWA), sds(8, n, Q_MEM),
                   sds(n, RNN_WIDTH), sds(CONV_WIDTH - 1, n, RNN_WIDTH), sds(n, RNN_WIDTH)],
        compiler_params=pltpu.CompilerParams(vmem_limit_bytes=VMEM_LIMIT),
        name="sample_pre",
    )(x, g_pre, w_in, conv, h0, conv_w, conv_b, wg, bg, lam)


def _sample_attn_kernel(sink_ref, q_ref, knew_ref, vnew_ref, qm_ref, ck_ref, cv_ref, cmk_ref, cmv_ref,
                        oswa_ref, omem_ref, nk_ref, nv_ref):
    sb, nh = q_ref.shape[0], q_ref.shape[1]
    rows = lambda b: slice(b * nh, (b + 1) * nh)
    stack = lambda f: jnp.concatenate([f(b) for b in range(sb)], axis=0)
    per_head = lambda ref, b: jnp.broadcast_to(ref[b:b + 1, :], (nh, ref.shape[1]))
    q = stack(lambda b: q_ref[b])
    sink = stack(lambda b: sink_ref[:, 0:1])
    s = stack(lambda b: _dot(q_ref[b].astype(BF16), ck_ref[b].astype(BF16)))
    s_new = jnp.sum(q * stack(lambda b: per_head(knew_ref, b)), axis=-1, keepdims=True)
    m = jnp.maximum(jnp.maximum(jnp.max(s, axis=-1, keepdims=True), s_new), sink)
    e = jnp.exp(s - m)
    e_new = jnp.exp(s_new - m)
    l = jnp.sum(e, axis=-1, keepdims=True) + e_new + jnp.exp(sink - m)
    o = stack(lambda b: _dot_nt(e[rows(b)].astype(BF16), cv_ref[b].astype(BF16)))
    o = (o + e_new * stack(lambda b: per_head(vnew_ref, b))) / l
    newest = lax.broadcasted_iota(jnp.int32, (1, WINDOW), 1) == WINDOW - 1
    kn_t = knew_ref[...].T
    vn_t = vnew_ref[...].T
    for b in range(sb):
        oswa_ref[b] = o[rows(b)]
        nk_ref[b] = jnp.where(newest, kn_t[:, b:b + 1], pltpu.roll(ck_ref[b], WINDOW - 1, 1))
        nv_ref[b] = jnp.where(newest, vn_t[:, b:b + 1], pltpu.roll(cv_ref[b], WINDOW - 1, 1))

    sm = stack(lambda b: _dot(qm_ref[b].astype(BF16), cmk_ref[b].astype(BF16)))
    em = jnp.exp(sm - jnp.max(sm, axis=-1, keepdims=True))
    lm = jnp.sum(em, axis=-1, keepdims=True)
    om = stack(lambda b: _dot_nt(em[rows(b)].astype(BF16), cmv_ref[b].astype(BF16))) / lm
    for b in range(sb):
        omem_ref[b] = om[rows(b)]


def _sample_attn(sink, q, knew, vnew, qm, ck, cv, cmk, cmv, sb):
    n = q.shape[0]
    blk = lambda *shape: pl.BlockSpec((sb,) + shape, lambda i: (i,) + (0,) * len(shape))
    sds = lambda *shape: jax.ShapeDtypeStruct(shape, F32)
    return pl.pallas_call(
        _sample_attn_kernel,
        grid=(n // sb,),
        in_specs=[_const_spec((SWA_HEADS, LANES)),
                  blk(SWA_HEADS, LANES), blk(KV_SWA), blk(KV_SWA), blk(8, Q_MEM),
                  blk(WINDOW, KV_SWA), blk(WINDOW, KV_SWA), blk(N_MEM, Q_MEM), blk(N_MEM, Q_MEM)],
        out_specs=[blk(SWA_HEADS, LANES), blk(8, Q_MEM), blk(WINDOW, KV_SWA), blk(WINDOW, KV_SWA)],
        out_shape=[sds(n, SWA_HEADS, LANES), sds(n, 8, Q_MEM),
                   sds(n, WINDOW, KV_SWA), sds(n, WINDOW, KV_SWA)],
        compiler_params=_params(1),
        name="sample_attn",
    )(sink, q, knew, vnew, qm, ck, cv, cmk, cmv)


def _sample_post_kernel(x_ref, oswa_ref, omem_ref, ornn_ref, wout_ref, gpost_ref, y_ref):
    lane = lax.broadcasted_iota(jnp.int32, (1, LANES), 1)
    lo = lane < HEAD_DIM
    tiles = []
    for pair in range(SWA_HEADS // 2):
        kv = (2 * pair) // SWA_GROUP
        low = oswa_ref[2 * pair]
        high = oswa_ref[2 * pair + 1]
        if kv == 1:
            low = pltpu.roll(low, HEAD_DIM, 1)
        else:
            high = pltpu.roll(high, HEAD_DIM, 1)
        tiles.append(jnp.where(lo, low, high))
    o_mem = jnp.zeros(omem_ref.shape[1:], F32)
    for head in range(MEM_HEADS):
        o_mem = jnp.where(_head_mask(Q_MEM, head), omem_ref[head], o_mem)
    ocat = jnp.concatenate(tiles + [o_mem, ornn_ref[...]], axis=-1).astype(BF16)
    o = _dot(ocat, wout_ref[...])
    y_ref[...] = x_ref[...] + _rms(o, gpost_ref[...])


def _sample_post(x, oswa, omem, ornn, w_out, g_post):
    return pl.pallas_call(
        _sample_post_kernel,
        out_shape=jax.ShapeDtypeStruct(x.shape, F32),
        compiler_params=pltpu.CompilerParams(vmem_limit_bytes=VMEM_LIMIT),
        name="sample_post",
    )(x, oswa, omem, ornn, w_out, g_post)


def _block_diag(w):
    nb, d, _ = w.shape
    eye = jnp.eye(nb, dtype=w.dtype)
    return (eye[:, None, :, None] * w[:, :, None, :]).reshape(nb * d, nb * d)


def _to_feature_major(a):
    n, tokens, heads, hd = a.shape
    return a.transpose(0, 2, 3, 1).reshape(n, heads * hd, tokens)


def _from_feature_major(a, heads):
    n, feat, tokens = a.shape
    return a.reshape(n, heads, feat // heads, tokens).transpose(0, 3, 1, 2)


def kernel(x_prompt, x_sample, mem_prompt, cache_swa_k, cache_swa_v, cache_mem_k, cache_mem_v, state_conv, state_rglru_h, ln_ffn1_pre, ln_ffn1_post, w_ffn1_in, w_ffn1_out, ln_mix_pre, ln_mix_post, w_in, w_out, swa_sinks, conv_w, conv_b, rg_wa, rg_ba, rg_wx, rg_bx, rg_lambda, ln_mem, w_mem_kv, ln_ffn2_pre, ln_ffn2_post, w_ffn2_in, w_ffn2_out):
    batch, seq, _ = x_prompt.shape
    n_dec = x_sample.shape[0]
    depth = w_in.shape[0]
    yp = x_prompt.reshape(batch * seq, D_MODEL)
    ys = x_sample.reshape(n_dec, D_MODEL)
    outs = [[] for _ in range(10)]
    for l in range(depth):
        row = lambda a: a[l].reshape(1, -1)
        w1i, w1o = w_ffn1_in[l].astype(BF16), w_ffn1_out[l].astype(BF16)
        w2i, w2o = w_ffn2_in[l].astype(BF16), w_ffn2_out[l].astype(BF16)
        wi, wo = w_in[l].astype(BF16), w_out[l].astype(BF16)
        w_tok = jnp.concatenate([wi[:, C_K:C_V], wi[:, C_XR:]], axis=1)
        w_feat = jnp.concatenate([wi[:, :C_K], wi[:, C_V:C_XR]], axis=1).T
        wg = jnp.concatenate([_block_diag(rg_wa[l]), _block_diag(rg_wx[l])], axis=1).astype(BF16)
        bg = jnp.concatenate([row(rg_ba), row(rg_bx)], axis=1)
        rnn = (conv_w[l], row(conv_b), wg, bg, row(rg_lambda))

        mk, mv = _memkv(mem_prompt, row(ln_mem), w_mem_kv[l].astype(BF16).T)
        yp = _ffn(yp, row(ln_ffn1_pre), row(ln_ffn1_post), w1i, w1o, FFN_TM)
        yp, kl, vl, cl, hl = _mix_prompt(yp, mk, mv, swa_sinks[l], row(ln_mix_pre), w_tok, w_feat, wo,
                                         row(ln_mix_post), *rnn, batch, seq, MIX_TQ)
        yp = _ffn(yp, row(ln_ffn2_pre), row(ln_ffn2_post), w2i, w2o, FFN_TM)
        outs[0].append(_from_feature_major(kl, SWA_KV_HEADS))
        outs[1].append(_from_feature_major(vl, SWA_KV_HEADS))
        outs[2].append(_from_feature_major(mk, MEM_HEADS))
        outs[3].append(_from_feature_major(mv, MEM_HEADS))
        outs[4].append(cl)
        outs[5].append(hl.reshape(batch, RNN_WIDTH))

        ys = _ffn(ys, row(ln_ffn1_pre), row(ln_ffn1_post), w1i, w1o, n_dec)
        qswa, knew, vnew, qmem, ornn, newconv, hnew = _sample_pre(
            ys, row(ln_mix_pre), wi, state_conv[l].transpose(1, 0, 2), state_rglru_h[l], *rnn)
        sink = jnp.broadcast_to(swa_sinks[l][:, None], (SWA_HEADS, LANES))
        oswa, omem, nk, nv = _sample_attn(
            sink, qswa.transpose(1, 0, 2), knew, vnew, qmem.transpose(1, 0, 2),
            _to_feature_major(cache_swa_k[l]), _to_feature_major(cache_swa_v[l]),
            _to_feature_major(cache_mem_k[l]), _to_feature_major(cache_mem_v[l]), SAMPLE_SB)
        ys = _sample_post(ys, oswa.transpose(1, 0, 2), omem.transpose(1, 0, 2), ornn, wo,
                          row(ln_mix_post))
        ys = _ffn(ys, row(ln_ffn2_pre), row(ln_ffn2_post), w2i, w2o, n_dec)
        outs[6].append(_from_feature_major(nk, SWA_KV_HEADS))
        outs[7].append(_from_feature_major(nv, SWA_KV_HEADS))
        outs[8].append(newconv.transpose(1, 0, 2))
        outs[9].append(hnew)
    return (yp.reshape(batch, seq, D_MODEL), ys.reshape(n_dec, 1, D_MODEL),
            *[jnp.stack(o) for o in outs])
```

```python
import jax
import jax.numpy as jnp
from jax import lax
from jax.experimental import pallas as pl
from jax.experimental.pallas import tpu as pltpu

F32 = jnp.float32
BF16 = jnp.bfloat16

D_MODEL = 1024
HEAD_DIM = 64
SWA_HEADS = 8
SWA_KV_HEADS = 2
SWA_GROUP = SWA_HEADS // SWA_KV_HEADS
WINDOW = 128
N_MEM = 256
MEM_HEADS = 4
RNN_WIDTH = 256
RNN_BLOCKS = 4
RNN_BLOCK_DIM = RNN_WIDTH // RNN_BLOCKS
CONV_WIDTH = 4
RG_C = 8.0
D_FF = 2816
EPS = 1e-6
NEG = -1e30

Q_SWA = SWA_HEADS * HEAD_DIM
KV_SWA = SWA_KV_HEADS * HEAD_DIM
Q_MEM = MEM_HEADS * HEAD_DIM
D_IN = Q_SWA + 2 * KV_SWA + Q_MEM + 2 * RNN_WIDTH
D_MIX = Q_SWA + Q_MEM + RNN_WIDTH
C_K = Q_SWA
C_V = C_K + KV_SWA
C_QM = C_V + KV_SWA
C_XR = C_QM + Q_MEM
C_GR = C_XR + RNN_WIDTH
SCALE = HEAD_DIM ** -0.5
LOG2E = 1.4426950408889634

D_TOK = KV_SWA + 2 * RNN_WIDTH
D_FEAT = Q_SWA + KV_SWA + Q_MEM
R_V = Q_SWA
R_QM = Q_SWA + KV_SWA
D_ATT = Q_SWA + Q_MEM

LANES = 128
SUBLANES = 8
VMEM_LIMIT = 56 * 1024 * 1024

FFN_TM = 512
FFN_SUB = 256
MIX_CHAINS = 2
MIX_TQ = 512
SAMPLE_SB = 8


def _dot(a, b):
    return jnp.dot(a, b, preferred_element_type=F32)


def _dot_nt(a, b):
    return lax.dot_general(a, b, (((1,), (1,)), ((), ())), preferred_element_type=F32)


def _dot_tn(a, b):
    return lax.dot_general(a, b, (((0,), (0,)), ((), ())), preferred_element_type=F32)


def _rms(x, g):
    return x * lax.rsqrt(jnp.mean(x * x, axis=-1, keepdims=True) + EPS) * g


def _softplus(x):
    return jnp.maximum(x, 0.0) + jnp.log1p(jnp.exp(-jnp.abs(x)))


def _const_spec(shape):
    return pl.BlockSpec(shape, lambda *_: (0,) * len(shape), pipeline_mode=pl.Buffered(1))


def _params(n_grid):
    return pltpu.CompilerParams(dimension_semantics=("arbitrary",) * n_grid,
                                vmem_limit_bytes=VMEM_LIMIT)


def _ffn_rows(x_ref, o_ref, gpre_ref, gpost_ref, win_ref, wout_ref):
    tm = x_ref.shape[0]
    sub = min(tm, FFN_SUB)
    g_half = 0.5 * gpost_ref[...]
    for r0 in range(0, tm, sub):
        x = x_ref[r0:r0 + sub, :]
        xn = _rms(x, gpre_ref[...]).astype(BF16)
        gate = _dot(xn, win_ref[:, :D_FF])
        up = _dot(xn, win_ref[:, D_FF:])
        act = (gate * jax.nn.sigmoid(gate) * up).astype(BF16)
        y = _dot(act, wout_ref[...])
        o_ref[r0:r0 + sub, :] = x + _rms(y, g_half)


def _ffn_kernel(xp_ref, xs_ref, gpre_ref, gpost_ref, win_ref, wout_ref, op_ref, os_ref):
    step = pl.program_id(0)
    last = pl.num_programs(0) - 1

    @pl.when(step < last)
    def _():
        _ffn_rows(xp_ref, op_ref, gpre_ref, gpost_ref, win_ref, wout_ref)

    @pl.when(step == last)
    def _():
        _ffn_rows(xs_ref, os_ref, gpre_ref, gpost_ref, win_ref, wout_ref)


def _ffn(xp, xs, g_pre, g_post, w_in, w_out, tm):
    m, ms = xp.shape[0], xs.shape[0]
    n = m // tm
    row = pl.BlockSpec((tm, D_MODEL), lambda i: (jnp.minimum(i, n - 1), 0))
    return pl.pallas_call(
        _ffn_kernel,
        grid=(n + 1,),
        in_specs=[row, _const_spec((ms, D_MODEL)), _const_spec((1, D_MODEL)), _const_spec((1, D_MODEL)),
                  _const_spec((D_MODEL, 2 * D_FF)), _const_spec((D_FF, D_MODEL))],
        out_specs=[row, pl.BlockSpec((ms, D_MODEL), lambda i: (0, 0))],
        out_shape=[jax.ShapeDtypeStruct((m, D_MODEL), F32), jax.ShapeDtypeStruct((ms, D_MODEL), F32)],
        compiler_params=_params(1),
        name="ffn",
    )(xp, xs, g_pre, g_post, w_in, w_out)


def _memkv_kernel(m_ref, g_ref, wt_ref, kt_ref, vt_ref):
    xn = _rms(m_ref[0], g_ref[...]).astype(BF16)
    kvt = _dot_nt(wt_ref[...], xn)
    kt_ref[0] = kvt[:Q_MEM, :]
    vt_ref[0] = kvt[Q_MEM:, :]


def _memkv(mem, g, wt):
    batch = mem.shape[0]
    out = pl.BlockSpec((1, Q_MEM, N_MEM), lambda i: (i, 0, 0))
    return pl.pallas_call(
        _memkv_kernel,
        grid=(batch,),
        in_specs=[pl.BlockSpec((1, N_MEM, D_MODEL), lambda i: (i, 0, 0)),
                  _const_spec((1, D_MODEL)), _const_spec((2 * Q_MEM, D_MODEL))],
        out_specs=[out, out],
        out_shape=[jax.ShapeDtypeStruct((batch, Q_MEM, N_MEM), F32)] * 2,
        compiler_params=_params(1),
        name="memkv",
    )(mem, g, wt)


def _rglru_gates(u, wg_ref, bg_ref, lam_ref):
    gates = jax.nn.sigmoid(_dot(u.astype(BF16), wg_ref[...]) + bg_ref[...])
    r = gates[:, :RNN_WIDTH]
    i = gates[:, RNN_WIDTH:]
    log_a = -RG_C * r * _softplus(-lam_ref[...])
    a = jnp.exp(log_a)
    b = jnp.sqrt(1.0 - a * a) * (i * u)
    return a, b


def _head_mask(width, head):
    lane = lax.broadcasted_iota(jnp.int32, (1, width), 1)
    return (lane >= head * HEAD_DIM) & (lane < (head + 1) * HEAD_DIM)


def _mix_prompt_kernel(sinks_ref, x_ref, mk_ref, mv_ref, *refs):
    n_shared = 10
    shared, outs, scratch = refs[:n_shared], refs[n_shared:n_shared + 5], refs[n_shared + 5:]
    bias_scr = scratch[4]
    gw = SWA_GROUP * WINDOW

    @pl.when((pl.program_id(0) == 0) & (pl.program_id(1) == 0))
    def _():
        ki = lax.broadcasted_iota(jnp.int32, (2 * WINDOW, gw), 0)
        qi = lax.broadcasted_iota(jnp.int32, (2 * WINDOW, gw), 1) & (WINDOW - 1)
        band = (ki >= qi) & (ki <= qi + WINDOW)
        bias_scr[0] = jnp.where(band, 0.0, NEG)
        bias_scr[1] = jnp.where(band & (ki >= WINDOW), 0.0, NEG)

    chains = []
    for c in range(x_ref.shape[1]):
        chain_scratch = [r if r is bias_scr else r.at[c] for r in scratch]
        chains.append(_mix_chain(sinks_ref, x_ref.at[0, c], mk_ref.at[0, c], mv_ref.at[0, c], *shared,
                                 *[r.at[0, c] for r in outs], *chain_scratch))
    for phase in range(3):
        for chain in chains:
            next(chain, None)


def _mix_chain(sinks_ref, x_ref, mk_ref, mv_ref, gpre_ref, wtok_ref, wfeat_ref, wout_ref,
               gpost_ref, convw_ref, convb_ref, wg_ref, bg_ref, lam_ref,
               y_ref, klast_ref, vlast_ref, convlast_ref, hlast_ref,
               k_scr, vt_scr, mkb_scr, mvt_scr, bias_scr, xr_scr, xtail_scr, h_scr,
               hs_scr, ps_scr, ot_scr):
    t = pl.program_id(1)
    tq = x_ref.shape[0]
    nblk = tq // WINDOW
    chunk = tq // SUBLANES
    gw = SWA_GROUP * WINDOW

    @pl.when(t == 0)
    def _():
        k_scr[0:WINDOW, :] = jnp.zeros((WINDOW, KV_SWA), BF16)
        vt_scr[:, 0:WINDOW] = jnp.zeros((KV_SWA, WINDOW), BF16)
        xtail_scr[...] = jnp.zeros(xtail_scr.shape, F32)
        h_scr[...] = jnp.zeros((1, RNN_WIDTH), F32)
        mkb_scr[...] = mk_ref[...].T.astype(BF16)
        mvt_scr[...] = mv_ref[...].astype(BF16)

    yield
    x = x_ref[...]
    xn = _rms(x, gpre_ref[...]).astype(BF16)
    ptok = _dot(xn, wtok_ref[...])
    k = ptok[:, :KV_SWA]
    k_scr[WINDOW:, :] = k.astype(BF16)

    def feat(r0, r1):
        return _dot_nt(wfeat_ref[r0:r1, :], xn)

    halves = RNN_WIDTH // LANES
    pitch = chunk + SUBLANES
    taps = CONV_WIDTH - 1

    def put(ref, val):
        for hf in range(halves):
            for c in range(SUBLANES):
                ref[hf, c * pitch:c * pitch + chunk, :] = (
                    val[c * chunk:(c + 1) * chunk, hf * LANES:(hf + 1) * LANES])

    def get(ref):
        return jnp.concatenate(
            [jnp.concatenate([ref[hf, c * pitch:c * pitch + chunk, :] for c in range(SUBLANES)], axis=0)
             for hf in range(halves)], axis=1)

    def get_perm(ref, r):
        return jnp.concatenate([ref[hf, pl.ds(r, SUBLANES, stride=pitch), :] for hf in range(halves)],
                               axis=1)

    def put_perm(ref, r, val):
        for hf in range(halves):
            ref[hf, pl.ds(r, SUBLANES, stride=pitch), :] = val[:, hf * LANES:(hf + 1) * LANES]

    put(xr_scr, ptok[:, KV_SWA:KV_SWA + RNN_WIDTH])
    xs = [get_perm(xr_scr, r) for r in range(chunk)]
    sub = lax.broadcasted_iota(jnp.int32, (SUBLANES, RNN_WIDTH), 0)
    head_rows = [pltpu.roll(jnp.where(sub == SUBLANES - 1, xtail_scr[i], xs[chunk - taps + i]), 1, 0)
                 for i in range(taps)]
    for i in range(taps):
        xtail_scr[i] = xs[chunk - taps + i]
    xm = head_rows + xs
    w = [convw_ref[jj:jj + 1, :] for jj in range(CONV_WIDTH)]
    us = []
    for r in range(chunk):
        u = convb_ref[...] + xm[r] * w[0]
        for jj in range(1, CONV_WIDTH):
            u = u + xm[r + jj] * w[jj]
        us.append(u)
    qt = [(feat(0, Q_SWA // 2) * (SCALE * LOG2E)).astype(BF16)]
    a, b = _rglru_gates(jnp.concatenate(us, axis=0), wg_ref, bg_ref, lam_ref)
    qt.append((feat(Q_SWA // 2, Q_SWA) * (SCALE * LOG2E)).astype(BF16))
    ps = hs = None
    for r in range(chunk):
        ar = a[r * SUBLANES:(r + 1) * SUBLANES, :]
        br = b[r * SUBLANES:(r + 1) * SUBLANES, :]
        if r == 0:
            ps, hs = ar, br
        else:
            hs = ar * hs + br
            ps = ar * ps
        put_perm(hs_scr, r, hs)
        put_perm(ps_scr, r, ps)
    h_in = h_scr[...]
    h_in_rows = []
    for c in range(SUBLANES):
        h_in_rows.append(jnp.broadcast_to(h_in, (chunk, RNN_WIDTH)))
        h_in = ps[c:c + 1, :] * h_in + hs[c:c + 1, :]
    h_scr[...] = h_in
    pvq = feat(R_V, D_FEAT)
    vt = pvq[:KV_SWA, :]
    vt_scr[:, WINDOW:] = vt.astype(BF16)
    qm = (pvq[KV_SWA:, :] * (SCALE * LOG2E)).astype(BF16)
    h = get(hs_scr) + get(ps_scr) * jnp.concatenate(h_in_rows, axis=0)
    o_rnn = (h * jax.nn.gelu(ptok[:, KV_SWA + RNN_WIDTH:])).astype(BF16)
    o_acc = _dot(o_rnn, wout_ref[D_ATT:, :])

    cols = []
    for head in range(MEM_HEADS):
        parts = [qm[head * HEAD_DIM:(head + 1) * HEAD_DIM, :]]
        if head > 0:
            parts.insert(0, jnp.zeros((head * HEAD_DIM, tq), BF16))
        if head < MEM_HEADS - 1:
            parts.append(jnp.zeros(((MEM_HEADS - 1 - head) * HEAD_DIM, tq), BF16))
        cols.append(jnp.concatenate(parts, axis=0))
    s_mem = _dot(mkb_scr[...], jnp.concatenate(cols, axis=1))

    def mem_softmax():
        m = jnp.max(s_mem, axis=0, keepdims=True)
        e = jnp.exp2(s_mem - m)
        return e.astype(BF16), 1.0 / jnp.sum(e, axis=0, keepdims=True)

    def mem_pv(eb, inv_l):
        for head in range(MEM_HEADS):
            o = _dot(mvt_scr[head * HEAD_DIM:(head + 1) * HEAD_DIM, :], eb[:, head * tq:(head + 1) * tq])
            ot_scr[Q_SWA + head * HEAD_DIM:Q_SWA + (head + 1) * HEAD_DIM, :] = (
                o * inv_l[:, head * tq:(head + 1) * tq]).astype(BF16)

    zero_head = jnp.zeros((HEAD_DIM, WINDOW), BF16)
    seg = lax.broadcasted_iota(jnp.int32, (1, gw), 1) // WINDOW
    sinks = []
    for kv in range(SWA_KV_HEADS):
        sink = jnp.zeros((1, gw), F32)
        for g in range(SWA_GROUP):
            sink = jnp.where(seg == g, sinks_ref[kv * SWA_GROUP + g] * LOG2E, sink)
        sinks.append(sink)

    def swa_scores(j):
        c0 = j * WINDOW
        bias = bias_scr[jnp.where(t == 0, 1, 0)] if j == 0 else bias_scr[0]
        out = []
        for kv in range(SWA_KV_HEADS):
            cols = []
            for g in range(SWA_GROUP):
                qh = qt[kv][g * HEAD_DIM:(g + 1) * HEAD_DIM, c0:c0 + WINDOW]
                cols.append(jnp.concatenate([qh, zero_head] if kv == 0 else [zero_head, qh], axis=0))
            out.append(_dot(k_scr[c0:c0 + 2 * WINDOW, :], jnp.concatenate(cols, axis=1)) + bias)
        return out

    def swa_finish(j, scores):
        c0 = j * WINDOW
        kvs = range(SWA_KV_HEADS)
        m = [jnp.maximum(jnp.max(scores[kv], axis=0, keepdims=True), sinks[kv]) for kv in kvs]
        e = [jnp.exp2(scores[kv] - m[kv]) for kv in kvs]
        inv_l = [1.0 / (jnp.sum(e[kv], axis=0, keepdims=True) + jnp.exp2(sinks[kv] - m[kv])) for kv in kvs]
        for kv in kvs:
            o = _dot(vt_scr[kv * HEAD_DIM:(kv + 1) * HEAD_DIM, c0:c0 + 2 * WINDOW], e[kv].astype(BF16))
            o = o * inv_l[kv]
            for g in range(SWA_GROUP):
                head = kv * SWA_GROUP + g
                ot_scr[head * HEAD_DIM:(head + 1) * HEAD_DIM, c0:c0 + WINDOW] = (
                    o[:, g * WINDOW:(g + 1) * WINDOW].astype(BF16))

    scores = swa_scores(0)
    for j in range(nblk):
        nxt = swa_scores(j + 1) if j + 1 < nblk else None
        swa_finish(j, scores)
        scores = nxt
        if j == 0:
            mem_p = mem_softmax()
        if j == min(1, nblk - 1):
            mem_pv(*mem_p)

    o = _dot_tn(ot_scr[...], wout_ref[:D_ATT, :]) + o_acc
    y_ref[...] = x + _rms(o, gpost_ref[...])

    k_scr[0:WINDOW, :] = k_scr[tq:tq + WINDOW, :]
    vt_scr[:, 0:WINDOW] = vt_scr[:, tq:tq + WINDOW]

    yield
    @pl.when(t == pl.num_programs(1) - 1)
    def _():
        klast_ref[...] = k[tq - WINDOW:, :].T
        vlast_ref[...] = vt[:, tq - WINDOW:]
        last = (SUBLANES - 1) * pitch + chunk
        convlast_ref[...] = jnp.concatenate(
            [xr_scr[hf, last - taps:last, :] for hf in range(halves)], axis=1)
        hlast_ref[...] = h_in


def _mix_prompt(x, mk, mv, sinks, g_pre, w_tok, w_feat, w_out, g_post, conv_w, conv_b, wg, bg, lam,
                batch, seq, tq):
    nt = seq // tq
    nc = MIX_CHAINS
    groups = batch // nc
    row = pl.BlockSpec((1, nc, tq, D_MODEL), lambda b, t: (b, 0, t, 0))
    per_b = lambda shape: pl.BlockSpec((1, nc) + shape, lambda b, t: (b, 0) + (0,) * len(shape))
    sds = lambda *shape: jax.ShapeDtypeStruct((groups, nc) + shape, F32)
    vmem = lambda shape, dtype: pltpu.VMEM((nc,) + shape, dtype)
    outs = pl.pallas_call(
        _mix_prompt_kernel,
        grid=(groups, nt),
        in_specs=[pl.BlockSpec(memory_space=pltpu.SMEM),
                  row, per_b((N_MEM, Q_MEM)), per_b((N_MEM, Q_MEM)),
                  _const_spec((1, D_MODEL)), _const_spec((D_MODEL, D_TOK)),
                  _const_spec((D_FEAT, D_MODEL)),
                  _const_spec((D_MIX, D_MODEL)), _const_spec((1, D_MODEL)),
                  _const_spec((CONV_WIDTH, RNN_WIDTH)), _const_spec((1, RNN_WIDTH)),
                  _const_spec((RNN_WIDTH, 2 * RNN_WIDTH)), _const_spec((1, 2 * RNN_WIDTH)),
                  _const_spec((1, RNN_WIDTH))],
        out_specs=[row, per_b((WINDOW, KV_SWA)), per_b((WINDOW, KV_SWA)),
                   per_b((CONV_WIDTH - 1, RNN_WIDTH)), per_b((1, RNN_WIDTH))],
        out_shape=[sds(seq, D_MODEL), sds(WINDOW, KV_SWA), sds(WINDOW, KV_SWA),
                   sds(CONV_WIDTH - 1, RNN_WIDTH), sds(1, RNN_WIDTH)],
        scratch_shapes=[vmem((WINDOW + tq, KV_SWA), BF16),
                        vmem((KV_SWA, WINDOW + tq), BF16),
                        vmem((N_MEM, Q_MEM), BF16),
                        vmem((Q_MEM, N_MEM), BF16),
                        pltpu.VMEM((2, 2 * WINDOW, SWA_GROUP * WINDOW), F32),
                        vmem((RNN_WIDTH // LANES, tq + SUBLANES * SUBLANES, LANES), F32),
                        vmem((CONV_WIDTH - 1, SUBLANES, RNN_WIDTH), F32),
                        vmem((1, RNN_WIDTH), F32),
                        vmem((RNN_WIDTH // LANES, tq + SUBLANES * SUBLANES, LANES), F32),
                        vmem((RNN_WIDTH // LANES, tq + SUBLANES * SUBLANES, LANES), F32),
                        vmem((D_ATT, tq), BF16)],
        compiler_params=_params(2),
        name="mix_prompt",
    )(sinks, x.reshape(groups, nc, seq, D_MODEL), mk.reshape(groups, nc, N_MEM, Q_MEM),
      mv.reshape(groups, nc, N_MEM, Q_MEM), g_pre, w_tok, w_feat, w_out, g_post, conv_w, conv_b, wg, bg, lam)
    y, kl, vl, cl, hl = outs
    return (y.reshape(batch * seq, D_MODEL), kl.reshape(batch, KV_SWA, WINDOW),
            vl.reshape(batch, KV_SWA, WINDOW), cl.reshape(batch, CONV_WIDTH - 1, RNN_WIDTH),
            hl.reshape(batch, 1, RNN_WIDTH))


def _sample_pre_kernel(x_ref, gpre_ref, win_ref, conv_ref, h0_ref, convw_ref, convb_ref,
                       wg_ref, bg_ref, lam_ref,
                       qswa_ref, knew_ref, vnew_ref, qmem_ref, ornn_ref, newconv_ref, hnew_ref):
    xn = _rms(x_ref[...], gpre_ref[...]).astype(BF16)
    proj = _dot(xn, win_ref[...])
    lane = lax.broadcasted_iota(jnp.int32, (1, LANES), 1)
    lo = lane < HEAD_DIM
    for head in range(SWA_HEADS):
        kv = head // SWA_GROUP
        c0 = (head // 2) * LANES
        qt = proj[:, c0:c0 + LANES] * SCALE
        if head % 2 != kv:
            qt = pltpu.roll(qt, HEAD_DIM, 1)
        keep = lo if kv == 0 else jnp.logical_not(lo)
        qswa_ref[head] = jnp.where(keep, qt, 0.0)
    knew_ref[...] = proj[:, C_K:C_V]
    vnew_ref[...] = proj[:, C_V:C_QM]
    qm = proj[:, C_QM:C_XR] * SCALE
    for head in range(MEM_HEADS):
        qmem_ref[head] = jnp.where(_head_mask(Q_MEM, head), qm, 0.0)
    for head in range(MEM_HEADS, 8):
        qmem_ref[head] = jnp.zeros_like(qm)

    xr = proj[:, C_XR:C_GR]
    u = convb_ref[...] + xr * convw_ref[CONV_WIDTH - 1:CONV_WIDTH, :]
    for jj in range(CONV_WIDTH - 1):
        u = u + conv_ref[jj] * convw_ref[jj:jj + 1, :]
    for jj in range(CONV_WIDTH - 2):
        newconv_ref[jj] = conv_ref[jj + 1]
    newconv_ref[CONV_WIDTH - 2] = xr
    a, b = _rglru_gates(u, wg_ref, bg_ref, lam_ref)
    h = a * h0_ref[...] + b
    hnew_ref[...] = h
    ornn_ref[...] = h * jax.nn.gelu(proj[:, C_GR:])


def _sample_pre(x, g_pre, w_in, conv, h0, conv_w, conv_b, wg, bg, lam):
    n = x.shape[0]
    sds = lambda *shape: jax.ShapeDtypeStruct(shape, F32)
    return pl.pallas_call(
        _sample_pre_kernel,
        out_shape=[sds(SWA_HEADS, n, LANES), sds(n, KV_SWA), sds(n, KV_SWA), sds(8, n, Q_MEM),
                   sds(n, RNN_WIDTH), sds(CONV_WIDTH - 1, n, RNN_WIDTH), sds(n, RNN_WIDTH)],
        compiler_params=pltpu.CompilerParams(vmem_limit_bytes=VMEM_LIMIT),
        name="sample_pre",
    )(x, g_pre, w_in, conv, h0, conv_w, conv_b, wg, bg, lam)


def _sample_attn_kernel(sink_ref, q_ref, knew_ref, vnew_ref, qm_ref, ck_ref, cv_ref, cmk_ref, cmv_ref,
                        oswa_ref, omem_ref, nk_ref, nv_ref):
    sb, nh = q_ref.shape[0], q_ref.shape[1]
    rows = lambda b: slice(b * nh, (b + 1) * nh)
    stack = lambda f: jnp.concatenate([f(b) for b in range(sb)], axis=0)
    per_head = lambda ref, b: jnp.broadcast_to(ref[b:b + 1, :], (nh, ref.shape[1]))
    q = stack(lambda b: q_ref[b])
    sink = stack(lambda b: sink_ref[:, 0:1])
    s = stack(lambda b: _dot(q_ref[b].astype(BF16), ck_ref[b].astype(BF16)))
    s_new = jnp.sum(q * stack(lambda b: per_head(knew_ref, b)), axis=-1, keepdims=True)
    m = jnp.maximum(jnp.maximum(jnp.max(s, axis=-1, keepdims=True), s_new), sink)
    e = jnp.exp(s - m)
    e_new = jnp.exp(s_new - m)
    l = jnp.sum(e, axis=-1, keepdims=True) + e_new + jnp.exp(sink - m)
    o = stack(lambda b: _dot_nt(e[rows(b)].astype(BF16), cv_ref[b].astype(BF16)))
    o = (o + e_new * stack(lambda b: per_head(vnew_ref, b))) / l
    newest = lax.broadcasted_iota(jnp.int32, (1, WINDOW), 1) == WINDOW - 1
    kn_t = knew_ref[...].T
    vn_t = vnew_ref[...].T
    for b in range(sb):
        oswa_ref[b] = o[rows(b)]
        nk_ref[b] = jnp.where(newest, kn_t[:, b:b + 1], pltpu.roll(ck_ref[b], WINDOW - 1, 1))
        nv_ref[b] = jnp.where(newest, vn_t[:, b:b + 1], pltpu.roll(cv_ref[b], WINDOW - 1, 1))

    sm = stack(lambda b: _dot(qm_ref[b].astype(BF16), cmk_ref[b].astype(BF16)))
    em = jnp.exp(sm - jnp.max(sm, axis=-1, keepdims=True))
    lm = jnp.sum(em, axis=-1, keepdims=True)
    om = stack(lambda b: _dot_nt(em[rows(b)].astype(BF16), cmv_ref[b].astype(BF16))) / lm
    for b in range(sb):
        omem_ref[b] = om[rows(b)]


def _sample_attn(sink, q, knew, vnew, qm, ck, cv, cmk, cmv, sb):
    n = q.shape[0]
    blk = lambda *shape: pl.BlockSpec((sb,) + shape, lambda i: (i,) + (0,) * len(shape))
    sds = lambda *shape: jax.ShapeDtypeStruct(shape, F32)
    return pl.pallas_call(
        _sample_attn_kernel,
        grid=(n // sb,),
        in_specs=[_const_spec((SWA_HEADS, LANES)),
                  blk(SWA_HEADS, LANES), blk(KV_SWA), blk(KV_SWA), blk(8, Q_MEM),
                  blk(WINDOW, KV_SWA), blk(WINDOW, KV_SWA), blk(N_MEM, Q_MEM), blk(N_MEM, Q_MEM)],
        out_specs=[blk(SWA_HEADS, LANES), blk(8, Q_MEM), blk(WINDOW, KV_SWA), blk(WINDOW, KV_SWA)],
        out_shape=[sds(n, SWA_HEADS, LANES), sds(n, 8, Q_MEM),
                   sds(n, WINDOW, KV_SWA), sds(n, WINDOW, KV_SWA)],
        compiler_params=_params(1),
        name="sample_attn",
    )(sink, q, knew, vnew, qm, ck, cv, cmk, cmv)


def _sample_post_kernel(x_ref, oswa_ref, omem_ref, ornn_ref, wout_ref, gpost_ref, y_ref):
    lane = lax.broadcasted_iota(jnp.int32, (1, LANES), 1)
    lo = lane < HEAD_DIM
    tiles = []
    for pair in range(SWA_HEADS // 2):
        kv = (2 * pair) // SWA_GROUP
        low = oswa_ref[2 * pair]
        high = oswa_ref[2 * pair + 1]
        if kv == 1:
            low = pltpu.roll(low, HEAD_DIM, 1)
        else:
            high = pltpu.roll(high, HEAD_DIM, 1)
        tiles.append(jnp.where(lo, low, high))
    o_mem = jnp.zeros(omem_ref.shape[1:], F32)
    for head in range(MEM_HEADS):
        o_mem = jnp.where(_head_mask(Q_MEM, head), omem_ref[head], o_mem)
    ocat = jnp.concatenate(tiles + [o_mem, ornn_ref[...]], axis=-1).astype(BF16)
    o = _dot(ocat, wout_ref[...])
    y_ref[...] = x_ref[...] + _rms(o, gpost_ref[...])


def _sample_post(x, oswa, omem, ornn, w_out, g_post):
    return pl.pallas_call(
        _sample_post_kernel,
        out_shape=jax.ShapeDtypeStruct(x.shape, F32),
        compiler_params=pltpu.CompilerParams(vmem_limit_bytes=VMEM_LIMIT),
        name="sample_post",
    )(x, oswa, omem, ornn, w_out, g_post)


def _block_diag(w):
    nb, d, _ = w.shape
    eye = jnp.eye(nb, dtype=w.dtype)
    return (eye[:, None, :, None] * w[:, :, None, :]).reshape(nb * d, nb * d)


def _to_feature_major(a):
    n, tokens, heads, hd = a.shape
    return a.transpose(0, 2, 3, 1).reshape(n, heads * hd, tokens)


def _from_feature_major(a, heads):
    n, feat, tokens = a.shape
    return a.reshape(n, heads, feat // heads, tokens).transpose(0, 3, 1, 2)


def kernel(x_prompt, x_sample, mem_prompt, cache_swa_k, cache_swa_v, cache_mem_k, cache_mem_v, state_conv, state_rglru_h, ln_ffn1_pre, ln_ffn1_post, w_ffn1_in, w_ffn1_out, ln_mix_pre, ln_mix_post, w_in, w_out, swa_sinks, conv_w, conv_b, rg_wa, rg_ba, rg_wx, rg_bx, rg_lambda, ln_mem, w_mem_kv, ln_ffn2_pre, ln_ffn2_post, w_ffn2_in, w_ffn2_out):
    batch, seq, _ = x_prompt.shape
    n_dec = x_sample.shape[0]
    depth = w_in.shape[0]
    yp = x_prompt.reshape(batch * seq, D_MODEL)
    ys = x_sample.reshape(n_dec, D_MODEL)
    outs = [[] for _ in range(10)]
    for l in range(depth):
        row = lambda a: a[l].reshape(1, -1)
        w1i, w1o = w_ffn1_in[l].astype(BF16), w_ffn1_out[l].astype(BF16)
        w2i, w2o = w_ffn2_in[l].astype(BF16), w_ffn2_out[l].astype(BF16)
        wi, wo = w_in[l].astype(BF16), w_out[l].astype(BF16)
        w_tok = jnp.concatenate([wi[:, C_K:C_V], wi[:, C_XR:]], axis=1)
        w_feat = jnp.concatenate([wi[:, :C_K], wi[:, C_V:C_XR]], axis=1).T
        wg = jnp.concatenate([_block_diag(rg_wa[l]), _block_diag(rg_wx[l])], axis=1).astype(BF16)
        bg = jnp.concatenate([row(rg_ba), row(rg_bx)], axis=1)
        rnn = (conv_w[l], row(conv_b), wg, bg, row(rg_lambda))

        yp, ys = _ffn(yp, ys, row(ln_ffn1_pre), row(ln_ffn1_post), w1i, w1o, FFN_TM)

        mk, mv = _memkv(mem_prompt, row(ln_mem), w_mem_kv[l].astype(BF16).T)
        yp, kl, vl, cl, hl = _mix_prompt(yp, mk, mv, swa_sinks[l], row(ln_mix_pre), w_tok, w_feat, wo,
                                         row(ln_mix_post), *rnn, batch, seq, MIX_TQ)
        outs[0].append(_from_feature_major(kl, SWA_KV_HEADS))
        outs[1].append(_from_feature_major(vl, SWA_KV_HEADS))
        outs[2].append(_from_feature_major(mk, MEM_HEADS))
        outs[3].append(_from_feature_major(mv, MEM_HEADS))
        outs[4].append(cl)
        outs[5].append(hl.reshape(batch, RNN_WIDTH))

        qswa, knew, vnew, qmem, ornn, newconv, hnew = _sample_pre(
            ys, row(ln_mix_pre), wi, state_conv[l].transpose(1, 0, 2), state_rglru_h[l], *rnn)
        sink = jnp.broadcast_to(swa_sinks[l][:, None], (SWA_HEADS, LANES))
        oswa, omem, nk, nv = _sample_attn(
            sink, qswa.transpose(1, 0, 2), knew, vnew, qmem.transpose(1, 0, 2),
            _to_feature_major(cache_swa_k[l]), _to_feature_major(cache_swa_v[l]),
            _to_feature_major(cache_mem_k[l]), _to_feature_major(cache_mem_v[l]), SAMPLE_SB)
        ys = _sample_post(ys, oswa.transpose(1, 0, 2), omem.transpose(1, 0, 2), ornn, wo,
                          row(ln_mix_post))
        outs[6].append(_from_feature_major(nk, SWA_KV_HEADS))
        outs[7].append(_from_feature_major(nv, SWA_KV_HEADS))
        outs[8].append(newconv.transpose(1, 0, 2))
        outs[9].append(hnew)

        yp, ys = _ffn(yp, ys, row(ln_ffn2_pre), row(ln_ffn2_post), w2i, w2o, FFN_TM)
    return (yp.reshape(batch, seq, D_MODEL), ys.reshape(n_dec, 1, D_MODEL),
            *[jnp.stack(o) for o in outs])
```

```python
import jax
import jax.numpy as jnp
from jax import lax
from jax.experimental import pallas as pl
from jax.experimental.pallas import tpu as pltpu

F32 = jnp.float32
BF16 = jnp.bfloat16

D_MODEL = 1024
HEAD_DIM = 64
SWA_HEADS = 8
SWA_KV_HEADS = 2
SWA_GROUP = SWA_HEADS // SWA_KV_HEADS
WINDOW = 128
N_MEM = 256
MEM_HEADS = 4
RNN_WIDTH = 256
RNN_BLOCKS = 4
RNN_BLOCK_DIM = RNN_WIDTH // RNN_BLOCKS
CONV_WIDTH = 4
RG_C = 8.0
D_FF = 2816
EPS = 1e-6
NEG = -1e30

Q_SWA = SWA_HEADS * HEAD_DIM
KV_SWA = SWA_KV_HEADS * HEAD_DIM
Q_MEM = MEM_HEADS * HEAD_DIM
D_IN = Q_SWA + 2 * KV_SWA + Q_MEM + 2 * RNN_WIDTH
D_MIX = Q_SWA + Q_MEM + RNN_WIDTH
C_K = Q_SWA
C_V = C_K + KV_SWA
C_QM = C_V + KV_SWA
C_XR = C_QM + Q_MEM
C_GR = C_XR + RNN_WIDTH
SCALE = HEAD_DIM ** -0.5
LOG2E = 1.4426950408889634

D_TOK = KV_SWA + 2 * RNN_WIDTH
D_FEAT = Q_SWA + KV_SWA + Q_MEM
R_V = Q_SWA
R_QM = Q_SWA + KV_SWA
D_ATT = Q_SWA + Q_MEM

LANES = 128
SUBLANES = 8
V7X_VMEM_BYTES = 64 * 1024 * 1024
VMEM_LIMIT = V7X_VMEM_BYTES * 7 // 8

FFN_TM = 512
FFN_SUB = 256
MIX_CHAINS = 2
MIX_TQ = 512
MEMKV_SEQS = 4
SAMPLE_SB = 16


def _dot(a, b):
    return jnp.dot(a, b, preferred_element_type=F32)


def _dot_nt(a, b):
    return lax.dot_general(a, b, (((1,), (1,)), ((), ())), preferred_element_type=F32)


def _dot_tn(a, b):
    return lax.dot_general(a, b, (((0,), (0,)), ((), ())), preferred_element_type=F32)


def _rms(x, g):
    return x * lax.rsqrt(jnp.mean(x * x, axis=-1, keepdims=True) + EPS) * g


def _softplus(x):
    return jnp.maximum(x, 0.0) + jnp.log1p(jnp.exp(-jnp.abs(x)))


def _const_spec(shape):
    return pl.BlockSpec(shape, lambda *_: (0,) * len(shape), pipeline_mode=pl.Buffered(1))


def _params(n_grid):
    return pltpu.CompilerParams(dimension_semantics=("arbitrary",) * n_grid,
                                vmem_limit_bytes=VMEM_LIMIT)


def _ffn_rows(x_ref, o_ref, gpre_ref, gpost_ref, win_ref, wout_ref):
    tm = x_ref.shape[0]
    sub = min(tm, FFN_SUB)
    g_half = 0.5 * gpost_ref[...]
    for r0 in range(0, tm, sub):
        x = x_ref[r0:r0 + sub, :]
        xn = _rms(x, gpre_ref[...]).astype(BF16)
        gate = _dot(xn, win_ref[:, :D_FF])
        up = _dot(xn, win_ref[:, D_FF:])
        act = (gate * jax.nn.sigmoid(gate) * up).astype(BF16)
        y = _dot(act, wout_ref[...])
        o_ref[r0:r0 + sub, :] = x + _rms(y, g_half)


def _ffn_kernel(xp_ref, xs_ref, gpre_ref, gpost_ref, win_ref, wout_ref, op_ref, os_ref):
    step = pl.program_id(0)
    last = pl.num_programs(0) - 1

    @pl.when(step < last)
    def _():
        _ffn_rows(xp_ref, op_ref, gpre_ref, gpost_ref, win_ref, wout_ref)

    @pl.when(step == last)
    def _():
        _ffn_rows(xs_ref, os_ref, gpre_ref, gpost_ref, win_ref, wout_ref)


def _ffn(xp, xs, g_pre, g_post, w_in, w_out, tm):
    m, ms = xp.shape[0], xs.shape[0]
    n = m // tm
    row = pl.BlockSpec((tm, D_MODEL), lambda i: (jnp.minimum(i, n - 1), 0))
    return pl.pallas_call(
        _ffn_kernel,
        grid=(n + 1,),
        in_specs=[row, _const_spec((ms, D_MODEL)), _const_spec((1, D_MODEL)), _const_spec((1, D_MODEL)),
                  _const_spec((D_MODEL, 2 * D_FF)), _const_spec((D_FF, D_MODEL))],
        out_specs=[row, pl.BlockSpec((ms, D_MODEL), lambda i: (0, 0))],
        out_shape=[jax.ShapeDtypeStruct((m, D_MODEL), F32), jax.ShapeDtypeStruct((ms, D_MODEL), F32)],
        compiler_params=_params(1),
        name="ffn",
    )(xp, xs, g_pre, g_post, w_in, w_out)


def _memkv_kernel(m_ref, g_ref, wt_ref, kt_ref, vt_ref):
    xn = _rms(m_ref[...], g_ref[...]).astype(BF16)
    kvt = _dot_nt(wt_ref[...], xn)
    for b in range(kt_ref.shape[0]):
        kt_ref[b] = kvt[:Q_MEM, b * N_MEM:(b + 1) * N_MEM]
        vt_ref[b] = kvt[Q_MEM:, b * N_MEM:(b + 1) * N_MEM]


def _memkv(mem, g, wt):
    batch = mem.shape[0]
    nb = MEMKV_SEQS
    out = pl.BlockSpec((nb, Q_MEM, N_MEM), lambda i: (i, 0, 0))
    return pl.pallas_call(
        _memkv_kernel,
        grid=(batch // nb,),
        in_specs=[pl.BlockSpec((nb * N_MEM, D_MODEL), lambda i: (i, 0)),
                  _const_spec((1, D_MODEL)), _const_spec((2 * Q_MEM, D_MODEL))],
        out_specs=[out, out],
        out_shape=[jax.ShapeDtypeStruct((batch, Q_MEM, N_MEM), F32)] * 2,
        compiler_params=_params(1),
        name="memkv",
    )(mem.reshape(batch * N_MEM, D_MODEL), g, wt)


def _rglru_gates(u, wg_ref, bg_ref, lam_ref):
    gates = jax.nn.sigmoid(_dot(u.astype(BF16), wg_ref[...]) + bg_ref[...])
    r = gates[:, :RNN_WIDTH]
    i = gates[:, RNN_WIDTH:]
    log_a = -RG_C * r * _softplus(-lam_ref[...])
    a = jnp.exp(log_a)
    b = jnp.sqrt(1.0 - a * a) * (i * u)
    return a, b


def _head_mask(width, head):
    lane = lax.broadcasted_iota(jnp.int32, (1, width), 1)
    return (lane >= head * HEAD_DIM) & (lane < (head + 1) * HEAD_DIM)


def _mix_prompt_kernel(sinks_ref, x_ref, mk_ref, mv_ref, *refs):
    n_shared = 10
    shared, outs, scratch = refs[:n_shared], refs[n_shared:n_shared + 5], refs[n_shared + 5:]
    bias_scr = scratch[4]
    gw = SWA_GROUP * WINDOW

    @pl.when((pl.program_id(0) == 0) & (pl.program_id(1) == 0))
    def _():
        ki = lax.broadcasted_iota(jnp.int32, (2 * WINDOW, gw), 0)
        qi = lax.broadcasted_iota(jnp.int32, (2 * WINDOW, gw), 1) & (WINDOW - 1)
        band = (ki >= qi) & (ki <= qi + WINDOW)
        bias_scr[0] = jnp.where(band, 0.0, NEG)
        bias_scr[1] = jnp.where(band & (ki >= WINDOW), 0.0, NEG)

    chains = []
    for c in range(x_ref.shape[1]):
        chain_scratch = [r if r is bias_scr else r.at[c] for r in scratch]
        chains.append(_mix_chain(sinks_ref, x_ref.at[0, c], mk_ref.at[0, c], mv_ref.at[0, c], *shared,
                                 *[r.at[0, c] for r in outs], *chain_scratch))
    for phase in range(3):
        for chain in chains:
            next(chain, None)


def _mix_chain(sinks_ref, x_ref, mk_ref, mv_ref, gpre_ref, wtok_ref, wfeat_ref, wout_ref,
               gpost_ref, convw_ref, convb_ref, wg_ref, bg_ref, lam_ref,
               y_ref, klast_ref, vlast_ref, convlast_ref, hlast_ref,
               k_scr, vt_scr, mkb_scr, mvt_scr, bias_scr, xr_scr, xtail_scr, h_scr,
               hs_scr, ps_scr, ot_scr):
    t = pl.program_id(1)
    tq = x_ref.shape[0]
    nblk = tq // WINDOW
    chunk = tq // SUBLANES
    gw = SWA_GROUP * WINDOW

    @pl.when(t == 0)
    def _():
        k_scr[0:WINDOW, :] = jnp.zeros((WINDOW, KV_SWA), BF16)
        vt_scr[:, 0:WINDOW] = jnp.zeros((KV_SWA, WINDOW), BF16)
        xtail_scr[...] = jnp.zeros(xtail_scr.shape, F32)
        h_scr[...] = jnp.zeros((1, RNN_WIDTH), F32)
        mkb_scr[...] = mk_ref[...].T.astype(BF16)
        mvt_scr[...] = mv_ref[...].astype(BF16)

    yield
    x = x_ref[...]
    xn = _rms(x, gpre_ref[...]).astype(BF16)
    ptok = _dot(xn, wtok_ref[...])
    k = ptok[:, :KV_SWA]
    k_scr[WINDOW:, :] = k.astype(BF16)

    def feat(r0, r1):
        return _dot_nt(wfeat_ref[r0:r1, :], xn)

    halves = RNN_WIDTH // LANES
    pitch = chunk + SUBLANES
    taps = CONV_WIDTH - 1

    def put(ref, val):
        for hf in range(halves):
            for c in range(SUBLANES):
                ref[hf, c * pitch:c * pitch + chunk, :] = (
                    val[c * chunk:(c + 1) * chunk, hf * LANES:(hf + 1) * LANES])

    def get(ref):
        return jnp.concatenate(
            [jnp.concatenate([ref[hf, c * pitch:c * pitch + chunk, :] for c in range(SUBLANES)], axis=0)
             for hf in range(halves)], axis=1)

    def get_perm(ref, r):
        return jnp.concatenate([ref[hf, pl.ds(r, SUBLANES, stride=pitch), :] for hf in range(halves)],
                               axis=1)

    def put_perm(ref, r, val):
        for hf in range(halves):
            ref[hf, pl.ds(r, SUBLANES, stride=pitch), :] = val[:, hf * LANES:(hf + 1) * LANES]

    put(xr_scr, ptok[:, KV_SWA:KV_SWA + RNN_WIDTH])
    xs = [get_perm(xr_scr, r) for r in range(chunk)]
    sub = lax.broadcasted_iota(jnp.int32, (SUBLANES, RNN_WIDTH), 0)
    head_rows = [pltpu.roll(jnp.where(sub == SUBLANES - 1, xtail_scr[i], xs[chunk - taps + i]), 1, 0)
                 for i in range(taps)]
    for i in range(taps):
        xtail_scr[i] = xs[chunk - taps + i]
    xm = head_rows + xs
    w = [convw_ref[jj:jj + 1, :] for jj in range(CONV_WIDTH)]
    us = []
    for r in range(chunk):
        u = convb_ref[...] + xm[r] * w[0]
        for jj in range(1, CONV_WIDTH):
            u = u + xm[r + jj] * w[jj]
        us.append(u)
    qt = [(feat(0, Q_SWA // 2) * (SCALE * LOG2E)).astype(BF16)]
    a, b = _rglru_gates(jnp.concatenate(us, axis=0), wg_ref, bg_ref, lam_ref)
    qt.append((feat(Q_SWA // 2, Q_SWA) * (SCALE * LOG2E)).astype(BF16))
    ps = hs = None
    for r in range(chunk):
        ar = a[r * SUBLANES:(r + 1) * SUBLANES, :]
        br = b[r * SUBLANES:(r + 1) * SUBLANES, :]
        if r == 0:
            ps, hs = ar, br
        else:
            hs = ar * hs + br
            ps = ar * ps
        put_perm(hs_scr, r, hs)
        put_perm(ps_scr, r, ps)
    h_in = h_scr[...]
    h_in_rows = []
    for c in range(SUBLANES):
        h_in_rows.append(jnp.broadcast_to(h_in, (chunk, RNN_WIDTH)))
        h_in = ps[c:c + 1, :] * h_in + hs[c:c + 1, :]
    h_scr[...] = h_in
    pvq = feat(R_V, D_FEAT)
    vt = pvq[:KV_SWA, :]
    vt_scr[:, WINDOW:] = vt.astype(BF16)
    qm = (pvq[KV_SWA:, :] * (SCALE * LOG2E)).astype(BF16)
    h = get(hs_scr) + get(ps_scr) * jnp.concatenate(h_in_rows, axis=0)
    o_rnn = (h * jax.nn.gelu(ptok[:, KV_SWA + RNN_WIDTH:])).astype(BF16)
    o_acc = _dot(o_rnn, wout_ref[D_ATT:, :])

    cols = []
    for head in range(MEM_HEADS):
        parts = [qm[head * HEAD_DIM:(head + 1) * HEAD_DIM, :]]
        if head > 0:
            parts.insert(0, jnp.zeros((head * HEAD_DIM, tq), BF16))
        if head < MEM_HEADS - 1:
            parts.append(jnp.zeros(((MEM_HEADS - 1 - head) * HEAD_DIM, tq), BF16))
        cols.append(jnp.concatenate(parts, axis=0))
    s_mem = _dot(mkb_scr[...], jnp.concatenate(cols, axis=1))

    def mem_softmax():
        m = jnp.max(s_mem, axis=0, keepdims=True)
        e = jnp.exp2(s_mem - m)
        return e.astype(BF16), 1.0 / jnp.sum(e, axis=0, keepdims=True)

    def mem_pv(eb, inv_l):
        for head in range(MEM_HEADS):
            o = _dot(mvt_scr[head * HEAD_DIM:(head + 1) * HEAD_DIM, :], eb[:, head * tq:(head + 1) * tq])
            ot_scr[Q_SWA + head * HEAD_DIM:Q_SWA + (head + 1) * HEAD_DIM, :] = (
                o * inv_l[:, head * tq:(head + 1) * tq]).astype(BF16)

    zero_head = jnp.zeros((HEAD_DIM, WINDOW), BF16)
    seg = lax.broadcasted_iota(jnp.int32, (1, gw), 1) // WINDOW
    sinks = []
    for kv in range(SWA_KV_HEADS):
        sink = jnp.zeros((1, gw), F32)
        for g in range(SWA_GROUP):
            sink = jnp.where(seg == g, sinks_ref[kv * SWA_GROUP + g] * LOG2E, sink)
        sinks.append(sink)

    def swa_scores(j):
        c0 = j * WINDOW
        bias = bias_scr[jnp.where(t == 0, 1, 0)] if j == 0 else bias_scr[0]
        out = []
        for kv in range(SWA_KV_HEADS):
            cols = []
            for g in range(SWA_GROUP):
                qh = qt[kv][g * HEAD_DIM:(g + 1) * HEAD_DIM, c0:c0 + WINDOW]
                cols.append(jnp.concatenate([qh, zero_head] if kv == 0 else [zero_head, qh], axis=0))
            out.append(_dot(k_scr[c0:c0 + 2 * WINDOW, :], jnp.concatenate(cols, axis=1)) + bias)
        return out

    def swa_finish(j, scores):
        c0 = j * WINDOW
        kvs = range(SWA_KV_HEADS)
        m = [jnp.maximum(jnp.max(scores[kv], axis=0, keepdims=True), sinks[kv]) for kv in kvs]
        e = [jnp.exp2(scores[kv] - m[kv]) for kv in kvs]
        inv_l = [1.0 / (jnp.sum(e[kv], axis=0, keepdims=True) + jnp.exp2(sinks[kv] - m[kv])) for kv in kvs]
        for kv in kvs:
            o = _dot(vt_scr[kv * HEAD_DIM:(kv + 1) * HEAD_DIM, c0:c0 + 2 * WINDOW], e[kv].astype(BF16))
            o = o * inv_l[kv]
            for g in range(SWA_GROUP):
                head = kv * SWA_GROUP + g
                ot_scr[head * HEAD_DIM:(head + 1) * HEAD_DIM, c0:c0 + WINDOW] = (
                    o[:, g * WINDOW:(g + 1) * WINDOW].astype(BF16))

    scores = swa_scores(0)
    for j in range(nblk):
        nxt = swa_scores(j + 1) if j + 1 < nblk else None
        swa_finish(j, scores)
        scores = nxt
        if j == 0:
            mem_p = mem_softmax()
        if j == min(1, nblk - 1):
            mem_pv(*mem_p)

    o = _dot_tn(ot_scr[...], wout_ref[:D_ATT, :]) + o_acc
    y_ref[...] = x + _rms(o, gpost_ref[...])

    k_scr[0:WINDOW, :] = k_scr[tq:tq + WINDOW, :]
    vt_scr[:, 0:WINDOW] = vt_scr[:, tq:tq + WINDOW]

    yield
    @pl.when(t == pl.num_programs(1) - 1)
    def _():
        klast_ref[...] = k[tq - WINDOW:, :].T
        vlast_ref[...] = vt[:, tq - WINDOW:]
        last = (SUBLANES - 1) * pitch + chunk
        convlast_ref[...] = jnp.concatenate(
            [xr_scr[hf, last - taps:last, :] for hf in range(halves)], axis=1)
        hlast_ref[...] = h_in


def _mix_prompt(x, mk, mv, sinks, g_pre, w_tok, w_feat, w_out, g_post, conv_w, conv_b, wg, bg, lam,
                batch, seq, tq):
    nt = seq // tq
    nc = MIX_CHAINS
    groups = batch // nc
    row = pl.BlockSpec((1, nc, tq, D_MODEL), lambda b, t: (b, 0, t, 0))
    per_b = lambda shape: pl.BlockSpec((1, nc) + shape, lambda b, t: (b, 0) + (0,) * len(shape))
    sds = lambda *shape: jax.ShapeDtypeStruct((groups, nc) + shape, F32)
    vmem = lambda shape, dtype: pltpu.VMEM((nc,) + shape, dtype)
    outs = pl.pallas_call(
        _mix_prompt_kernel,
        grid=(groups, nt),
        in_specs=[pl.BlockSpec(memory_space=pltpu.SMEM),
                  row, per_b((N_MEM, Q_MEM)), per_b((N_MEM, Q_MEM)),
                  _const_spec((1, D_MODEL)), _const_spec((D_MODEL, D_TOK)),
                  _const_spec((D_FEAT, D_MODEL)),
                  _const_spec((D_MIX, D_MODEL)), _const_spec((1, D_MODEL)),
                  _const_spec((CONV_WIDTH, RNN_WIDTH)), _const_spec((1, RNN_WIDTH)),
                  _const_spec((RNN_WIDTH, 2 * RNN_WIDTH)), _const_spec((1, 2 * RNN_WIDTH)),
                  _const_spec((1, RNN_WIDTH))],
        out_specs=[row, per_b((WINDOW, KV_SWA)), per_b((WINDOW, KV_SWA)),
                   per_b((CONV_WIDTH - 1, RNN_WIDTH)), per_b((1, RNN_WIDTH))],
        out_shape=[sds(seq, D_MODEL), sds(WINDOW, KV_SWA), sds(WINDOW, KV_SWA),
                   sds(CONV_WIDTH - 1, RNN_WIDTH), sds(1, RNN_WIDTH)],
        scratch_shapes=[vmem((WINDOW + tq, KV_SWA), BF16),
                        vmem((KV_SWA, WINDOW + tq), BF16),
                        vmem((N_MEM, Q_MEM), BF16),
                        vmem((Q_MEM, N_MEM), BF16),
                        pltpu.VMEM((2, 2 * WINDOW, SWA_GROUP * WINDOW), F32),
                        vmem((RNN_WIDTH // LANES, tq + SUBLANES * SUBLANES, LANES), F32),
                        vmem((CONV_WIDTH - 1, SUBLANES, RNN_WIDTH), F32),
                        vmem((1, RNN_WIDTH), F32),
                        vmem((RNN_WIDTH // LANES, tq + SUBLANES * SUBLANES, LANES), F32),
                        vmem((RNN_WIDTH // LANES, tq + SUBLANES * SUBLANES, LANES), F32),
                        vmem((D_ATT, tq), BF16)],
        compiler_params=_params(2),
        name="mix_prompt",
    )(sinks, x.reshape(groups, nc, seq, D_MODEL), mk.reshape(groups, nc, N_MEM, Q_MEM),
      mv.reshape(groups, nc, N_MEM, Q_MEM), g_pre, w_tok, w_feat, w_out, g_post, conv_w, conv_b, wg, bg, lam)
    y, kl, vl, cl, hl = outs
    return (y.reshape(batch * seq, D_MODEL), kl.reshape(batch, KV_SWA, WINDOW),
            vl.reshape(batch, KV_SWA, WINDOW), cl.reshape(batch, CONV_WIDTH - 1, RNN_WIDTH),
            hl.reshape(batch, 1, RNN_WIDTH))


def _sample_pre_kernel(x_ref, gpre_ref, win_ref, conv_ref, h0_ref, convw_ref, convb_ref,
                       wg_ref, bg_ref, lam_ref,
                       qswa_ref, knew_ref, vnew_ref, qmem_ref, ornn_ref, newconv_ref, hnew_ref):
    xn = _rms(x_ref[...], gpre_ref[...]).astype(BF16)
    proj = _dot(xn, win_ref[...])
    lane = lax.broadcasted_iota(jnp.int32, (1, LANES), 1)
    lo = lane < HEAD_DIM
    for head in range(SWA_HEADS):
        kv = head // SWA_GROUP
        c0 = (head // 2) * LANES
        qt = proj[:, c0:c0 + LANES] * SCALE
        if head % 2 != kv:
            qt = pltpu.roll(qt, HEAD_DIM, 1)
        keep = lo if kv == 0 else jnp.logical_not(lo)
        qswa_ref[:, head, :] = jnp.where(keep, qt, 0.0)
    knew_ref[...] = proj[:, C_K:C_V]
    vnew_ref[...] = proj[:, C_V:C_QM]
    qm = proj[:, C_QM:C_XR] * SCALE
    for head in range(MEM_HEADS):
        qmem_ref[:, head, :] = jnp.where(_head_mask(Q_MEM, head), qm, 0.0)
    for head in range(MEM_HEADS, 8):
        qmem_ref[:, head, :] = jnp.zeros_like(qm)

    xr = proj[:, C_XR:C_GR]
    u = convb_ref[...] + xr * convw_ref[CONV_WIDTH - 1:CONV_WIDTH, :]
    for jj in range(CONV_WIDTH - 1):
        u = u + conv_ref[jj] * convw_ref[jj:jj + 1, :]
    for jj in range(CONV_WIDTH - 2):
        newconv_ref[jj] = conv_ref[jj + 1]
    newconv_ref[CONV_WIDTH - 2] = xr
    a, b = _rglru_gates(u, wg_ref, bg_ref, lam_ref)
    h = a * h0_ref[...] + b
    hnew_ref[...] = h
    ornn_ref[...] = h * jax.nn.gelu(proj[:, C_GR:])


def _sample_pre(x, g_pre, w_in, conv, h0, conv_w, conv_b, wg, bg, lam):
    n = x.shape[0]
    sds = lambda *shape: jax.ShapeDtypeStruct(shape, F32)
    return pl.pallas_call(
        _sample_pre_kernel,
        out_shape=[sds(n, SWA_HEADS, LANES), sds(n, KV_SWA), sds(n, KV_SWA), sds(n, 8, Q_MEM),
                   sds(n, RNN_WIDTH), sds(CONV_WIDTH - 1, n, RNN_WIDTH), sds(n, RNN_WIDTH)],
        compiler_params=pltpu.CompilerParams(vmem_limit_bytes=VMEM_LIMIT),
        name="sample_pre",
    )(x, g_pre, w_in, conv, h0, conv_w, conv_b, wg, bg, lam)


def _sample_attn_kernel(sink_ref, q_ref, knew_ref, vnew_ref, qm_ref, ck_ref, cv_ref, cmk_ref, cmv_ref,
                        oswa_ref, omem_ref, nk_ref, nv_ref):
    sb, nh = q_ref.shape[0], q_ref.shape[1]
    rows = lambda b: slice(b * nh, (b + 1) * nh)
    stack = lambda f: jnp.concatenate([f(b) for b in range(sb)], axis=0)
    per_head = lambda ref, b: jnp.broadcast_to(ref[b:b + 1, :], (nh, ref.shape[1]))
    q = stack(lambda b: q_ref[b])
    sink = stack(lambda b: sink_ref[:, 0:1])
    s = stack(lambda b: _dot(q_ref[b].astype(BF16), ck_ref[b].astype(BF16)))
    s_new = jnp.sum(q * stack(lambda b: per_head(knew_ref, b)), axis=-1, keepdims=True)
    m = jnp.maximum(jnp.maximum(jnp.max(s, axis=-1, keepdims=True), s_new), sink)
    e = jnp.exp(s - m)
    e_new = jnp.exp(s_new - m)
    l = jnp.sum(e, axis=-1, keepdims=True) + e_new + jnp.exp(sink - m)
    o = stack(lambda b: _dot_nt(e[rows(b)].astype(BF16), cv_ref[b].astype(BF16)))
    o = (o + e_new * stack(lambda b: per_head(vnew_ref, b))) / l
    newest = lax.broadcasted_iota(jnp.int32, (1, WINDOW), 1) == WINDOW - 1
    kn_t = knew_ref[...].T
    vn_t = vnew_ref[...].T
    for b in range(sb):
        oswa_ref[b] = o[rows(b)]
        nk_ref[b] = jnp.where(newest, kn_t[:, b:b + 1], pltpu.roll(ck_ref[b], WINDOW - 1, 1))
        nv_ref[b] = jnp.where(newest, vn_t[:, b:b + 1], pltpu.roll(cv_ref[b], WINDOW - 1, 1))

    sm = stack(lambda b: _dot(qm_ref[b].astype(BF16), cmk_ref[b].astype(BF16)))
    em = jnp.exp(sm - jnp.max(sm, axis=-1, keepdims=True))
    lm = jnp.sum(em, axis=-1, keepdims=True)
    om = stack(lambda b: _dot_nt(em[rows(b)].astype(BF16), cmv_ref[b].astype(BF16))) / lm
    for b in range(sb):
        omem_ref[b] = om[rows(b)]


def _sample_attn(sink, q, knew, vnew, qm, ck, cv, cmk, cmv, sb):
    n = q.shape[0]
    blk = lambda *shape: pl.BlockSpec((sb,) + shape, lambda i: (i,) + (0,) * len(shape))
    sds = lambda *shape: jax.ShapeDtypeStruct(shape, F32)
    return pl.pallas_call(
        _sample_attn_kernel,
        grid=(n // sb,),
        in_specs=[_const_spec((SWA_HEADS, LANES)),
                  blk(SWA_HEADS, LANES), blk(KV_SWA), blk(KV_SWA), blk(8, Q_MEM),
                  blk(WINDOW, KV_SWA), blk(WINDOW, KV_SWA), blk(N_MEM, Q_MEM), blk(N_MEM, Q_MEM)],
        out_specs=[blk(SWA_HEADS, LANES), blk(8, Q_MEM), blk(WINDOW, KV_SWA), blk(WINDOW, KV_SWA)],
        out_shape=[sds(n, SWA_HEADS, LANES), sds(n, 8, Q_MEM),
                   sds(n, WINDOW, KV_SWA), sds(n, WINDOW, KV_SWA)],
        compiler_params=_params(1),
        name="sample_attn",
    )(sink, q, knew, vnew, qm, ck, cv, cmk, cmv)


def _sample_post_kernel(x_ref, oswa_ref, omem_ref, ornn_ref, wout_ref, gpost_ref, y_ref):
    lane = lax.broadcasted_iota(jnp.int32, (1, LANES), 1)
    lo = lane < HEAD_DIM
    tiles = []
    for pair in range(SWA_HEADS // 2):
        kv = (2 * pair) // SWA_GROUP
        low = oswa_ref[:, 2 * pair, :]
        high = oswa_ref[:, 2 * pair + 1, :]
        if kv == 1:
            low = pltpu.roll(low, HEAD_DIM, 1)
        else:
            high = pltpu.roll(high, HEAD_DIM, 1)
        tiles.append(jnp.where(lo, low, high))
    o_mem = jnp.zeros((omem_ref.shape[0], Q_MEM), F32)
    for head in range(MEM_HEADS):
        o_mem = jnp.where(_head_mask(Q_MEM, head), omem_ref[:, head, :], o_mem)
    ocat = jnp.concatenate(tiles + [o_mem, ornn_ref[...]], axis=-1).astype(BF16)
    o = _dot(ocat, wout_ref[...])
    y_ref[...] = x_ref[...] + _rms(o, gpost_ref[...])


def _sample_post(x, oswa, omem, ornn, w_out, g_post):
    return pl.pallas_call(
        _sample_post_kernel,
        out_shape=jax.ShapeDtypeStruct(x.shape, F32),
        compiler_params=pltpu.CompilerParams(vmem_limit_bytes=VMEM_LIMIT),
        name="sample_post",
    )(x, oswa, omem, ornn, w_out, g_post)


def _block_diag(w):
    nb, d, _ = w.shape
    eye = jnp.eye(nb, dtype=w.dtype)
    return (eye[:, None, :, None] * w[:, :, None, :]).reshape(nb * d, nb * d)


def _to_feature_major(a):
    n, tokens, heads, hd = a.shape
    return a.transpose(0, 2, 3, 1).reshape(n, heads * hd, tokens)


def _from_feature_major(a, heads):
    n, feat, tokens = a.shape
    return a.reshape(n, heads, feat // heads, tokens).transpose(0, 3, 1, 2)


def kernel(x_prompt, x_sample, mem_prompt, cache_swa_k, cache_swa_v, cache_mem_k, cache_mem_v, state_conv, state_rglru_h, ln_ffn1_pre, ln_ffn1_post, w_ffn1_in, w_ffn1_out, ln_mix_pre, ln_mix_post, w_in, w_out, swa_sinks, conv_w, conv_b, rg_wa, rg_ba, rg_wx, rg_bx, rg_lambda, ln_mem, w_mem_kv, ln_ffn2_pre, ln_ffn2_post, w_ffn2_in, w_ffn2_out):
    batch, seq, _ = x_prompt.shape
    n_dec = x_sample.shape[0]
    depth = w_in.shape[0]
    yp = x_prompt.reshape(batch * seq, D_MODEL)
    ys = x_sample.reshape(n_dec, D_MODEL)
    outs = [[] for _ in range(10)]
    for l in range(depth):
        row = lambda a: a[l].reshape(1, -1)
        w1i, w1o = w_ffn1_in[l].astype(BF16), w_ffn1_out[l].astype(BF16)
        w2i, w2o = w_ffn2_in[l].astype(BF16), w_ffn2_out[l].astype(BF16)
        wi, wo = w_in[l].astype(BF16), w_out[l].astype(BF16)
        w_tok = jnp.concatenate([wi[:, C_K:C_V], wi[:, C_XR:]], axis=1)
        w_feat = jnp.concatenate([wi[:, :C_K], wi[:, C_V:C_XR]], axis=1).T
        wg = jnp.concatenate([_block_diag(rg_wa[l]), _block_diag(rg_wx[l])], axis=1).astype(BF16)
        bg = jnp.concatenate([row(rg_ba), row(rg_bx)], axis=1)
        rnn = (conv_w[l], row(conv_b), wg, bg, row(rg_lambda))

        yp, ys = _ffn(yp, ys, row(ln_ffn1_pre), row(ln_ffn1_post), w1i, w1o, FFN_TM)

        mk, mv = _memkv(mem_prompt, row(ln_mem), w_mem_kv[l].astype(BF16).T)
        yp, kl, vl, cl, hl = _mix_prompt(yp, mk, mv, swa_sinks[l], row(ln_mix_pre), w_tok, w_feat, wo,
                                         row(ln_mix_post), *rnn, batch, seq, MIX_TQ)
        outs[0].append(_from_feature_major(kl, SWA_KV_HEADS))
        outs[1].append(_from_feature_major(vl, SWA_KV_HEADS))
        outs[2].append(_from_feature_major(mk, MEM_HEADS))
        outs[3].append(_from_feature_major(mv, MEM_HEADS))
        outs[4].append(cl)
        outs[5].append(hl.reshape(batch, RNN_WIDTH))

        qswa, knew, vnew, qmem, ornn, newconv, hnew = _sample_pre(
            ys, row(ln_mix_pre), wi, state_conv[l].transpose(1, 0, 2), state_rglru_h[l], *rnn)
        sink = jnp.broadcast_to(swa_sinks[l][:, None], (SWA_HEADS, LANES))
        oswa, omem, nk, nv = _sample_attn(
            sink, qswa, knew, vnew, qmem,
            _to_feature_major(cache_swa_k[l]), _to_feature_major(cache_swa_v[l]),
            _to_feature_major(cache_mem_k[l]), _to_feature_major(cache_mem_v[l]), SAMPLE_SB)
        ys = _sample_post(ys, oswa, omem, ornn, wo,
                          row(ln_mix_post))
        outs[6].append(_from_feature_major(nk, SWA_KV_HEADS))
        outs[7].append(_from_feature_major(nv, SWA_KV_HEADS))
        outs[8].append(newconv.transpose(1, 0, 2))
        outs[9].append(hnew)

        yp, ys = _ffn(yp, ys, row(ln_ffn2_pre), row(ln_ffn2_post), w2i, w2o, FFN_TM)
    return (yp.reshape(batch, seq, D_MODEL), ys.reshape(n_dec, 1, D_MODEL),
            *[jnp.stack(o) for o in outs])
```
